```python
import jax, jax.numpy as jnp
from jax import lax
import numpy as np

D_MODEL = 1024
BATCH = 16
SEQ = 2048
DEPTH = 1

D_MIX = D_MODEL
D_CONV = D_MIX // 2
D_RET = D_MIX - D_CONV
N_RET_HEADS = 4
RET_HEAD_DIM = D_RET // N_RET_HEADS
CONV_WIDTH = 31
CHUNK = 128
D_FF = ((8 * D_MODEL // 3 + 255) // 256) * 256
D_PLE = 256
ROPE_BASE = 10000.0
EPS = 1e-6
D_IN = 2 * D_CONV + 4 * D_RET

kernel_name = "hymba_conformer_retention_block"


def rmsnorm(x, w):
    xf = x.astype(jnp.float32)
    y = xf * lax.rsqrt(jnp.mean(xf * xf, axis=-1, keepdims=True) + EPS)
    return (y * w.astype(jnp.float32)).astype(x.dtype)


def layernorm(x, w, b):
    xf = x.astype(jnp.float32)
    mu = jnp.mean(xf, axis=-1, keepdims=True)
    var = jnp.mean(jnp.square(xf - mu), axis=-1, keepdims=True)
    y = (xf - mu) * lax.rsqrt(var + EPS)
    return (y * w.astype(jnp.float32) + b.astype(jnp.float32)).astype(x.dtype)


def rotary(t, pos):
    half = t.shape[-1] // 2
    freqs = ROPE_BASE ** (-jnp.arange(half, dtype=jnp.float32) / half)
    ang = pos[:, None] * freqs[None, :]
    cos = jnp.cos(ang)[None, :, None, :]
    sin = jnp.sin(ang)[None, :, None, :]
    t1, t2 = t[..., :half], t[..., half:]
    return jnp.concatenate([t1 * cos - t2 * sin, t1 * sin + t2 * cos], axis=-1)


def conformer_conv(u_a, u_b, dw_w, dw_b, ln_w, ln_b):
    u = u_a * jax.nn.sigmoid(u_b)
    y = lax.conv_general_dilated(
        u, dw_w[:, None, :].astype(u.dtype), window_strides=(1,),
        padding=[(CONV_WIDTH - 1, 0)],
        dimension_numbers=("NWC", "WIO", "NWC"),
        feature_group_count=D_CONV)
    y = y + dw_b
    y = layernorm(y, ln_w, ln_b)
    return jax.nn.silu(y)


def chunkwise_retention(q, k, v):
    b, s, h, dh = q.shape
    nc = s // CHUNK
    log_g = jnp.log(1.0 - 2.0 ** (-5.0 - jnp.arange(h, dtype=jnp.float32)))
    idx = jnp.arange(CHUNK, dtype=jnp.float32)
    rel = idx[:, None] - idx[None, :]
    decay_in = jnp.where(rel >= 0, jnp.exp(log_g[:, None, None] * jnp.maximum(rel, 0.0)), 0.0)
    q_decay = jnp.exp(log_g[:, None] * (idx + 1.0))[None, :, :, None]
    k_decay = jnp.exp(log_g[:, None] * (CHUNK - 1.0 - idx))[None, :, :, None]
    chunk_decay = jnp.exp(log_g * CHUNK)[None, :, None, None]

    def to_chunks(t):
        return t.reshape(b, nc, CHUNK, h, dh).transpose(1, 0, 3, 2, 4)

    def step(state, qkv):
        qc, kc, vc = qkv
        scores = jnp.einsum("bhid,bhjd->bhij", qc, kc) * decay_in[None]
        inner = jnp.einsum("bhij,bhjd->bhid", scores, vc)
        cross = jnp.einsum("bhid,bhde->bhie", qc, state) * q_decay
        new_state = state * chunk_decay + jnp.einsum("bhjd,bhje->bhde", kc * k_decay, vc)
        return new_state, inner + cross

    state0 = jnp.zeros((b, h, dh, dh), jnp.float32)
    _, out = lax.scan(step, state0, (to_chunks(q), to_chunks(k), to_chunks(v)))
    return out.transpose(1, 0, 3, 2, 4).reshape(b, s, h, dh)


def retention_group(hq, hk, hv, hg):
    b, s, _ = hq.shape
    shp = (b, s, N_RET_HEADS, RET_HEAD_DIM)
    pos = jnp.arange(s, dtype=jnp.float32)
    q = rotary(hq.reshape(shp).astype(jnp.float32), pos)
    k = rotary(hk.reshape(shp).astype(jnp.float32), pos) * (RET_HEAD_DIM ** -0.5)
    v = hv.reshape(shp).astype(jnp.float32)
    o = chunkwise_retention(q, k, v)
    o = o * lax.rsqrt(jnp.mean(o * o, axis=-1, keepdims=True) + EPS)
    o = o.reshape(b, s, D_RET).astype(hg.dtype)
    return jax.nn.silu(hg) * o


def setup_inputs(seed: int = 0) -> dict:
    key = jax.random.key(seed)
    ks = jax.random.split(key, 20)
    f32 = jnp.float32

    def nrm(k, shape, scale):
        return jax.random.normal(k, shape, f32) * scale

    def gain(k, shape):
        return 1.0 + 0.01 * jax.random.normal(k, shape, f32)

    L = DEPTH
    return {
        "x": nrm(ks[0], (BATCH, SEQ, D_MODEL), 1.0),
        "p": nrm(ks[1], (DEPTH, BATCH, SEQ, D_PLE), 1.0),
        "mix_pre_norm": gain(ks[2], (L, D_MODEL)),
        "w_in": nrm(ks[3], (L, D_MODEL, D_IN), D_MODEL ** -0.5),
        "conv_dw_w": nrm(ks[4], (L, CONV_WIDTH, D_CONV), CONV_WIDTH ** -0.5),
        "conv_dw_b": nrm(ks[5], (L, D_CONV), 0.01),
        "conv_ln_w": gain(ks[6], (L, D_CONV)),
        "conv_ln_b": nrm(ks[7], (L, D_CONV), 0.01),
        "w_out": nrm(ks[8], (L, D_MIX, D_MODEL), D_MIX ** -0.5),
        "mix_post_norm": gain(ks[9], (L, D_MODEL)),
        "ffn_pre_norm": gain(ks[10], (L, D_MODEL)),
        "w_ffn_gate": nrm(ks[11], (L, D_MODEL, D_FF), D_MODEL ** -0.5),
        "w_ffn_up": nrm(ks[12], (L, D_MODEL, D_FF), D_MODEL ** -0.5),
        "w_ffn_down": nrm(ks[13], (L, D_FF, D_MODEL), D_FF ** -0.5),
        "ffn_post_norm": gain(ks[14], (L, D_MODEL)),
        "w_ple_gate": nrm(ks[15], (L, D_MODEL, D_MODEL), D_MODEL ** -0.5),
        "w_ple_proj": nrm(ks[16], (L, D_PLE, D_MODEL), D_PLE ** -0.5),
        "ple_post_norm": gain(ks[17], (L, D_MODEL)),
    }


def reference(x, p, mix_pre_norm, w_in, conv_dw_w, conv_dw_b, conv_ln_w, conv_ln_b,
              w_out, mix_post_norm, ffn_pre_norm, w_ffn_gate, w_ffn_up, w_ffn_down,
              ffn_post_norm, w_ple_gate, w_ple_proj, ple_post_norm):
    for i in range(DEPTH):
        h = rmsnorm(x, mix_pre_norm[i])
        u = h @ w_in[i]
        c0, c1 = D_CONV, 2 * D_CONV
        u_a, u_b = u[..., :c0], u[..., c0:c1]
        hq = u[..., c1:c1 + D_RET]
        hk = u[..., c1 + D_RET:c1 + 2 * D_RET]
        hv = u[..., c1 + 2 * D_RET:c1 + 3 * D_RET]
        hg = u[..., c1 + 3 * D_RET:]
        y_conv = conformer_conv(u_a, u_b, conv_dw_w[i], conv_dw_b[i], conv_ln_w[i], conv_ln_b[i])
        y_ret = retention_group(hq, hk, hv, hg)
        y = jnp.concatenate([y_conv, y_ret], axis=-1) @ w_out[i]
        x = x + rmsnorm(y, mix_post_norm[i])
        h = rmsnorm(x, ffn_pre_norm[i])
        f = (jax.nn.silu(h @ w_ffn_gate[i]) * (h @ w_ffn_up[i])) @ w_ffn_down[i]
        x = x + rmsnorm(f, ffn_post_norm[i])
        e = jax.nn.sigmoid(x @ w_ple_gate[i]) * (p[i].astype(x.dtype) @ w_ple_proj[i])
        x = x + rmsnorm(e, ple_post_norm[i])
    return x
```

```python
import functools

import jax
import jax.numpy as jnp
from jax import lax
from jax.experimental import pallas as pl
from jax.experimental.pallas import tpu as pltpu

D_MODEL = 1024
D_CONV = 512
D_RET = 512
N_HEADS = 4
HEAD_DIM = 128
CONV_WIDTH = 31
CHUNK = 128
D_FF = 2816
D_PLE = 256
D_IN = 2 * D_CONV + 4 * D_RET
ROPE_BASE = 10000.0
EPS = 1e-6

SUBLANES = 8
SEQ_TILE = 256
HALO = 32
CONV_ROWS = 32
VMEM_LIMIT_BYTES = 56 * 1024 * 1024

Q0 = 2 * D_CONV
K0 = Q0 + D_RET
V0 = K0 + D_RET
G0 = V0 + D_RET

F32 = jnp.float32
BF16 = jnp.bfloat16


def _rmsnorm(x, w):
    ms = jnp.mean(x * x, axis=-1, keepdims=True)
    return (x * lax.rsqrt(ms + EPS)) * w


def _silu(x):
    return x * jax.nn.sigmoid(x)


def _dot(a, b):
    return jnp.dot(a, b, preferred_element_type=F32)


def _block_kernel(cd_ref, x_ref, p_ref, cos_ref, sin_ref, dmask_ref, qd_ref, kd_ref,
                  mix_pre_ref, w_in_ref, dw_w_ref, dw_b_ref, ln_w_ref, ln_b_ref,
                  w_out_ref, mix_post_ref, ffn_pre_ref, w_gate_ref, w_up_ref,
                  w_down_ref, ffn_post_ref, w_pg_ref, w_pp_ref, ple_post_ref,
                  o_ref,
                  u_ref, glu_ref, mix_ref, act_ref, state_ref):
    ts = x_ref.shape[0]

    @pl.when(pl.program_id(1) == 0)
    def _():
        state_ref[...] = jnp.zeros_like(state_ref)
        glu_ref[0:HALO, :] = jnp.zeros((HALO, D_CONV), F32)

    x = x_ref[...]
    h = _rmsnorm(x, mix_pre_ref[...]).astype(BF16)
    u_ref[...] = _dot(h, w_in_ref[...])

    glu_ref[HALO:HALO + ts, :] = u_ref[:, 0:D_CONV] * jax.nn.sigmoid(u_ref[:, D_CONV:2 * D_CONV])
    for r0 in range(0, ts, CONV_ROWS):
        acc = jnp.broadcast_to(dw_b_ref[...], (CONV_ROWS, D_CONV))
        for k in range(CONV_WIDTH):
            rows = glu_ref[pl.ds(HALO + r0 - (CONV_WIDTH - 1) + k, CONV_ROWS), :]
            acc = acc + rows * dw_w_ref[k:k + 1, :]
        mu = jnp.mean(acc, axis=-1, keepdims=True)
        cen = acc - mu
        var = jnp.mean(cen * cen, axis=-1, keepdims=True)
        yn = (cen * lax.rsqrt(var + EPS)) * ln_w_ref[...] + ln_b_ref[...]
        mix_ref[r0:r0 + CONV_ROWS, 0:D_CONV] = _silu(yn).astype(BF16)
    glu_ref[0:HALO, :] = glu_ref[ts:ts + HALO, :]

    scale = HEAD_DIM ** -0.5
    for c0 in range(0, ts, CHUNK):
        cosf = cos_ref[c0:c0 + CHUNK, :]
        sinf = sin_ref[c0:c0 + CHUNK, :]
        for hd in range(N_HEADS):
            lo = hd * HEAD_DIM
            q = u_ref[c0:c0 + CHUNK, Q0 + lo:Q0 + lo + HEAD_DIM]
            k = u_ref[c0:c0 + CHUNK, K0 + lo:K0 + lo + HEAD_DIM]
            v = u_ref[c0:c0 + CHUNK, V0 + lo:V0 + lo + HEAD_DIM].astype(BF16)
            g = u_ref[c0:c0 + CHUNK, G0 + lo:G0 + lo + HEAD_DIM]
            q = q * cosf + pltpu.roll(q, HEAD_DIM // 2, 1) * sinf
            k = (k * cosf + pltpu.roll(k, HEAD_DIM // 2, 1) * sinf) * scale
            qb = q.astype(BF16)
            scores = lax.dot_general(qb, k.astype(BF16), (((1,), (1,)), ((), ())),
                                     preferred_element_type=F32)
            scores = scores * dmask_ref[hd]
            state = state_ref[hd]
            inner = _dot(scores.astype(BF16), v)
            cross = _dot(qb, state.astype(BF16)) * qd_ref[:, lo:lo + HEAD_DIM]
            kdec = (k * kd_ref[:, lo:lo + HEAD_DIM]).astype(BF16)
            state_ref[hd] = state * cd_ref[hd] + lax.dot_general(
                kdec, v, (((0,), (0,)), ((), ())), preferred_element_type=F32)
            o = inner + cross
            o = o * lax.rsqrt(jnp.mean(o * o, axis=-1, keepdims=True) + EPS)
            mix_ref[c0:c0 + CHUNK, D_CONV + lo:D_CONV + lo + HEAD_DIM] = (_silu(g) * o).astype(BF16)

    y = _dot(mix_ref[...], w_out_ref[...])
    x = x + _rmsnorm(y, mix_post_ref[...])

    h = _rmsnorm(x, ffn_pre_ref[...]).astype(BF16)
    gate = _dot(h, w_gate_ref[...])
    up = _dot(h, w_up_ref[...])
    act_ref[...] = (_silu(gate) * up).astype(BF16)
    f = _dot(act_ref[...], w_down_ref[...])
    x = x + _rmsnorm(f, ffn_post_ref[...])

    e = jax.nn.sigmoid(_dot(x.astype(BF16), w_pg_ref[...])) * _dot(p_ref[...].astype(BF16), w_pp_ref[...])
    o_ref[...] = x + _rmsnorm(e, ple_post_ref[...])


def _retention_tables(seq):
    half = HEAD_DIM // 2
    pos = jnp.arange(seq, dtype=F32)
    freqs = ROPE_BASE ** (-jnp.arange(half, dtype=F32) / half)
    ang = pos[:, None] * freqs[None, :]
    cos, sin = jnp.cos(ang), jnp.sin(ang)
    cosf = jnp.concatenate([cos, cos], axis=-1)
    sinf = jnp.concatenate([-sin, sin], axis=-1)
    log_g = jnp.log(1.0 - 2.0 ** (-5.0 - jnp.arange(N_HEADS, dtype=F32)))
    idx = jnp.arange(CHUNK, dtype=F32)
    rel = idx[:, None] - idx[None, :]
    dmask = jnp.where(rel >= 0, jnp.exp(log_g[:, None, None] * jnp.maximum(rel, 0.0)), 0.0)
    q_decay = jnp.exp(log_g[:, None] * (idx + 1.0))
    k_decay = jnp.exp(log_g[:, None] * (CHUNK - 1.0 - idx))
    chunk_decay = jnp.exp(log_g * CHUNK)
    widen = lambda t: jnp.repeat(t.T, HEAD_DIM, axis=1)
    return cosf, sinf, dmask, widen(q_decay), widen(k_decay), chunk_decay


def kernel(x, p, mix_pre_norm, w_in, conv_dw_w, conv_dw_b, conv_ln_w, conv_ln_b, w_out,
           mix_post_norm, ffn_pre_norm, w_ffn_gate, w_ffn_up, w_ffn_down, ffn_post_norm,
           w_ple_gate, w_ple_proj, ple_post_norm):
    batch, seq, d_model = x.shape
    depth = p.shape[0]
    assert depth == 1 and d_model == D_MODEL and seq % SEQ_TILE == 0 and SEQ_TILE % CHUNK == 0
    ts = SEQ_TILE
    cosf, sinf, dmask, qd, kd, cd = _retention_tables(seq)

    row = lambda v: v.reshape(1, -1)
    wb = lambda w: w[0].astype(BF16)

    def tile_spec(width):
        return pl.BlockSpec((None, ts, width), lambda b, j: (b, j, 0))

    def const_spec(shape):
        zeros = (0,) * len(shape)
        return pl.BlockSpec(shape, lambda b, j: zeros, pipeline_mode=pl.Buffered(1))

    operands = [
        (cd, pl.BlockSpec(memory_space=pltpu.SMEM)),
        (x, tile_spec(D_MODEL)),
        (p[0], tile_spec(D_PLE)),
        (cosf, pl.BlockSpec((ts, HEAD_DIM), lambda b, j: (j, 0))),
        (sinf, pl.BlockSpec((ts, HEAD_DIM), lambda b, j: (j, 0))),
        (dmask, const_spec((N_HEADS, CHUNK, CHUNK))),
        (qd, const_spec((CHUNK, D_RET))),
        (kd, const_spec((CHUNK, D_RET))),
        (row(mix_pre_norm[0]), const_spec((1, D_MODEL))),
        (wb(w_in), const_spec((D_MODEL, D_IN))),
        (conv_dw_w[0], const_spec((CONV_WIDTH, D_CONV))),
        (row(conv_dw_b[0]), const_spec((1, D_CONV))),
        (row(conv_ln_w[0]), const_spec((1, D_CONV))),
        (row(conv_ln_b[0]), const_spec((1, D_CONV))),
        (wb(w_out), const_spec((D_MODEL, D_MODEL))),
        (row(mix_post_norm[0]), const_spec((1, D_MODEL))),
        (row(ffn_pre_norm[0]), const_spec((1, D_MODEL))),
        (wb(w_ffn_gate), const_spec((D_MODEL, D_FF))),
        (wb(w_ffn_up), const_spec((D_MODEL, D_FF))),
        (wb(w_ffn_down), const_spec((D_FF, D_MODEL))),
        (row(ffn_post_norm[0]), const_spec((1, D_MODEL))),
        (wb(w_ple_gate), const_spec((D_MODEL, D_MODEL))),
        (wb(w_ple_proj), const_spec((D_PLE, D_MODEL))),
        (row(ple_post_norm[0]), const_spec((1, D_MODEL))),
    ]
    args = [a for a, _ in operands]
    in_specs = [s for _, s in operands]

    return pl.pallas_call(
        _block_kernel,
        grid=(batch, seq // ts),
        in_specs=in_specs,
        out_specs=tile_spec(D_MODEL),
        out_shape=jax.ShapeDtypeStruct((batch, seq, D_MODEL), x.dtype),
        scratch_shapes=[
            pltpu.VMEM((ts, D_IN), F32),
            pltpu.VMEM((HALO + ts, D_CONV), F32),
            pltpu.VMEM((ts, D_MODEL), BF16),
            pltpu.VMEM((ts, D_FF), BF16),
            pltpu.VMEM((N_HEADS, HEAD_DIM, HEAD_DIM), F32),
        ],
        compiler_params=pltpu.CompilerParams(
            dimension_semantics=("arbitrary", "arbitrary"),
            vmem_limit_bytes=VMEM_LIMIT_BYTES,
        ),
        name="hymba_block",
    )(*args)
```

```python
import functools

import jax
import jax.numpy as jnp
from jax import lax
from jax.experimental import pallas as pl
from jax.experimental.pallas import tpu as pltpu

D_MODEL = 1024
D_CONV = 512
D_RET = 512
N_HEADS = 4
HEAD_DIM = 128
CONV_WIDTH = 31
CHUNK = 128
D_FF = 2816
D_PLE = 256
D_IN = 2 * D_CONV + 4 * D_RET
ROPE_BASE = 10000.0
EPS = 1e-6

SUBLANES = 8
SEQ_TILE = 256
HALO = 32
CONV_ROWS = 32
VMEM_LIMIT_BYTES = 56 * 1024 * 1024

Q0 = 2 * D_CONV
K0 = Q0 + D_RET
V0 = K0 + D_RET
G0 = V0 + D_RET

F32 = jnp.float32
BF16 = jnp.bfloat16


def _rmsnorm(x, w):
    ms = jnp.mean(x * x, axis=-1, keepdims=True)
    return (x * lax.rsqrt(ms + EPS)) * w


def _silu(x):
    return x * jax.nn.sigmoid(x)


def _dot(a, b):
    return jnp.dot(a, b, preferred_element_type=F32)


def _block_kernel(cd_ref, x_ref, p_ref, cos_ref, sin_ref, dmask_ref, qd_ref, kd_ref,
                  mix_pre_ref, w_in_ref, dw_w_ref, dw_b_ref, ln_w_ref, ln_b_ref,
                  w_out_ref, mix_post_ref, ffn_pre_ref, w_gate_ref, w_up_ref,
                  w_down_ref, ffn_post_ref, w_pg_ref, w_pp_ref, ple_post_ref,
                  o_ref,
                  u_ref, glu_ref, mix_ref, act_ref, state_ref):
    ts = x_ref.shape[0]

    @pl.when(pl.program_id(1) == 0)
    def _():
        state_ref[...] = jnp.zeros_like(state_ref)
        glu_ref[0, 0:HALO, :] = jnp.zeros((HALO, D_CONV), F32)

    x = x_ref[...]
    h = _rmsnorm(x, mix_pre_ref[...]).astype(BF16)
    u_ref[...] = _dot(h, w_in_ref[...])

    glu_ref[0, HALO:HALO + ts, :] = u_ref[:, 0:D_CONV] * jax.nn.sigmoid(u_ref[:, D_CONV:2 * D_CONV])
    for b in range(1, SUBLANES):
        glu_ref[b, SUBLANES:HALO + ts, :] = glu_ref[0, SUBLANES - b:HALO + ts - b, :]
    groups = CONV_ROWS // SUBLANES
    for r0 in range(0, ts, CONV_ROWS):
        accs = [dw_b_ref[...]] * groups
        for dist in range(CONV_WIDTH):
            a, b = divmod(dist, SUBLANES)
            w_tap = dw_w_ref[CONV_WIDTH - 1 - dist]
            start = HALO + r0 - SUBLANES * a
            for i in range(groups):
                lo = start + SUBLANES * i
                accs[i] = accs[i] + glu_ref[b, lo:lo + SUBLANES, :] * w_tap
        acc = jnp.concatenate(accs, axis=0)
        mu = jnp.mean(acc, axis=-1, keepdims=True)
        cen = acc - mu
        var = jnp.mean(cen * cen, axis=-1, keepdims=True)
        yn = (cen * lax.rsqrt(var + EPS)) * ln_w_ref[...] + ln_b_ref[...]
        mix_ref[r0:r0 + CONV_ROWS, 0:D_CONV] = _silu(yn).astype(BF16)
    glu_ref[0, 0:HALO, :] = glu_ref[0, ts:ts + HALO, :]

    scale = HEAD_DIM ** -0.5
    for c0 in range(0, ts, CHUNK):
        cosf = cos_ref[c0:c0 + CHUNK, :]
        sinf = sin_ref[c0:c0 + CHUNK, :]
        for hd in range(N_HEADS):
            lo = hd * HEAD_DIM
            q = u_ref[c0:c0 + CHUNK, Q0 + lo:Q0 + lo + HEAD_DIM]
            k = u_ref[c0:c0 + CHUNK, K0 + lo:K0 + lo + HEAD_DIM]
            v = u_ref[c0:c0 + CHUNK, V0 + lo:V0 + lo + HEAD_DIM].astype(BF16)
            g = u_ref[c0:c0 + CHUNK, G0 + lo:G0 + lo + HEAD_DIM]
            q = q * cosf + pltpu.roll(q, HEAD_DIM // 2, 1) * sinf
            k = (k * cosf + pltpu.roll(k, HEAD_DIM // 2, 1) * sinf) * scale
            qb = q.astype(BF16)
            scores = lax.dot_general(qb, k.astype(BF16), (((1,), (1,)), ((), ())),
                                     preferred_element_type=F32)
            scores = scores * dmask_ref[hd]
            state = state_ref[hd]
            inner = _dot(scores.astype(BF16), v)
            cross = _dot(qb, state.astype(BF16)) * qd_ref[:, lo:lo + HEAD_DIM]
            kdec = (k * kd_ref[:, lo:lo + HEAD_DIM]).astype(BF16)
            state_ref[hd] = state * cd_ref[hd] + lax.dot_general(
                kdec, v, (((0,), (0,)), ((), ())), preferred_element_type=F32)
            o = inner + cross
            o = o * lax.rsqrt(jnp.mean(o * o, axis=-1, keepdims=True) + EPS)
            mix_ref[c0:c0 + CHUNK, D_CONV + lo:D_CONV + lo + HEAD_DIM] = (_silu(g) * o).astype(BF16)

    y = _dot(mix_ref[...], w_out_ref[...])
    x = x + _rmsnorm(y, mix_post_ref[...])

    h = _rmsnorm(x, ffn_pre_ref[...]).astype(BF16)
    gate = _dot(h, w_gate_ref[...])
    up = _dot(h, w_up_ref[...])
    act_ref[...] = (_silu(gate) * up).astype(BF16)
    f = _dot(act_ref[...], w_down_ref[...])
    x = x + _rmsnorm(f, ffn_post_ref[...])

    e = jax.nn.sigmoid(_dot(x.astype(BF16), w_pg_ref[...])) * _dot(p_ref[...].astype(BF16), w_pp_ref[...])
    o_ref[...] = x + _rmsnorm(e, ple_post_ref[...])


def _retention_tables(seq):
    half = HEAD_DIM // 2
    pos = jnp.arange(seq, dtype=F32)
    freqs = ROPE_BASE ** (-jnp.arange(half, dtype=F32) / half)
    ang = pos[:, None] * freqs[None, :]
    cos, sin = jnp.cos(ang), jnp.sin(ang)
    cosf = jnp.concatenate([cos, cos], axis=-1)
    sinf = jnp.concatenate([-sin, sin], axis=-1)
    log_g = jnp.log(1.0 - 2.0 ** (-5.0 - jnp.arange(N_HEADS, dtype=F32)))
    idx = jnp.arange(CHUNK, dtype=F32)
    rel = idx[:, None] - idx[None, :]
    dmask = jnp.where(rel >= 0, jnp.exp(log_g[:, None, None] * jnp.maximum(rel, 0.0)), 0.0)
    q_decay = jnp.exp(log_g[:, None] * (idx + 1.0))
    k_decay = jnp.exp(log_g[:, None] * (CHUNK - 1.0 - idx))
    chunk_decay = jnp.exp(log_g * CHUNK)
    widen = lambda t: jnp.repeat(t.T, HEAD_DIM, axis=1)
    return cosf, sinf, dmask, widen(q_decay), widen(k_decay), chunk_decay


def kernel(x, p, mix_pre_norm, w_in, conv_dw_w, conv_dw_b, conv_ln_w, conv_ln_b, w_out,
           mix_post_norm, ffn_pre_norm, w_ffn_gate, w_ffn_up, w_ffn_down, ffn_post_norm,
           w_ple_gate, w_ple_proj, ple_post_norm):
    batch, seq, d_model = x.shape
    depth = p.shape[0]
    assert depth == 1 and d_model == D_MODEL and seq % SEQ_TILE == 0 and SEQ_TILE % CHUNK == 0
    ts = SEQ_TILE
    cosf, sinf, dmask, qd, kd, cd = _retention_tables(seq)

    row = lambda v: v.reshape(1, -1)
    wb = lambda w: w[0].astype(BF16)

    def tile_spec(width):
        return pl.BlockSpec((None, ts, width), lambda b, j: (b, j, 0))

    def const_spec(shape):
        zeros = (0,) * len(shape)
        return pl.BlockSpec(shape, lambda b, j: zeros, pipeline_mode=pl.Buffered(1))

    operands = [
        (cd, pl.BlockSpec(memory_space=pltpu.SMEM)),
        (x, tile_spec(D_MODEL)),
        (p[0], tile_spec(D_PLE)),
        (cosf, pl.BlockSpec((ts, HEAD_DIM), lambda b, j: (j, 0))),
        (sinf, pl.BlockSpec((ts, HEAD_DIM), lambda b, j: (j, 0))),
        (dmask, const_spec((N_HEADS, CHUNK, CHUNK))),
        (qd, const_spec((CHUNK, D_RET))),
        (kd, const_spec((CHUNK, D_RET))),
        (row(mix_pre_norm[0]), const_spec((1, D_MODEL))),
        (wb(w_in), const_spec((D_MODEL, D_IN))),
        (jnp.broadcast_to(conv_dw_w[0][:, None, :], (CONV_WIDTH, SUBLANES, D_CONV)),
         const_spec((CONV_WIDTH, SUBLANES, D_CONV))),
        (jnp.broadcast_to(conv_dw_b[0][None, :], (SUBLANES, D_CONV)), const_spec((SUBLANES, D_CONV))),
        (row(conv_ln_w[0]), const_spec((1, D_CONV))),
        (row(conv_ln_b[0]), const_spec((1, D_CONV))),
        (wb(w_out), const_spec((D_MODEL, D_MODEL))),
        (row(mix_post_norm[0]), const_spec((1, D_MODEL))),
        (row(ffn_pre_norm[0]), const_spec((1, D_MODEL))),
        (wb(w_ffn_gate), const_spec((D_MODEL, D_FF))),
        (wb(w_ffn_up), const_spec((D_MODEL, D_FF))),
        (wb(w_ffn_down), const_spec((D_FF, D_MODEL))),
        (row(ffn_post_norm[0]), const_spec((1, D_MODEL))),
        (wb(w_ple_gate), const_spec((D_MODEL, D_MODEL))),
        (wb(w_ple_proj), const_spec((D_PLE, D_MODEL))),
        (row(ple_post_norm[0]), const_spec((1, D_MODEL))),
    ]
    args = [a for a, _ in operands]
    in_specs = [s for _, s in operands]

    return pl.pallas_call(
        _block_kernel,
        grid=(batch, seq // ts),
        in_specs=in_specs,
        out_specs=tile_spec(D_MODEL),
        out_shape=jax.ShapeDtypeStruct((batch, seq, D_MODEL), x.dtype),
        scratch_shapes=[
            pltpu.VMEM((ts, D_IN), F32),
            pltpu.VMEM((SUBLANES, HALO + ts, D_CONV), F32),
            pltpu.VMEM((ts, D_MODEL), BF16),
            pltpu.VMEM((ts, D_FF), BF16),
            pltpu.VMEM((N_HEADS, HEAD_DIM, HEAD_DIM), F32),
        ],
        compiler_params=pltpu.CompilerParams(
            dimension_semantics=("arbitrary", "arbitrary"),
            vmem_limit_bytes=VMEM_LIMIT_BYTES,
        ),
        name="hymba_block",
    )(*args)
```

```python
import functools

import jax
import jax.numpy as jnp
from jax import lax
from jax.experimental import pallas as pl
from jax.experimental.pallas import tpu as pltpu

D_MODEL = 1024
D_CONV = 512
D_RET = 512
N_HEADS = 4
HEAD_DIM = 128
CONV_WIDTH = 31
CHUNK = 128
D_FF = 2816
D_PLE = 256
D_IN = 2 * D_CONV + 4 * D_RET
ROPE_BASE = 10000.0
EPS = 1e-6

SUBLANES = 8
SEQ_TILE = 256
HALO = 32
CONV_ROWS = 32
FFN_COLS = 256
DOWN_COLS = 256
VMEM_LIMIT_BYTES = 56 * 1024 * 1024

SEGMENT_ORDER = (
    "CMMC "
    "CMCMCCMCCMCCMCCMC "
    "CMCMCMCM "
    "MC "
    "CMMCM"
)

Q0 = 2 * D_CONV
K0 = Q0 + D_RET
V0 = K0 + D_RET
G0 = V0 + D_RET

F32 = jnp.float32
BF16 = jnp.bfloat16
_EXHAUSTED = object()


def _rmsnorm(x, w):
    ms = jnp.mean(x * x, axis=-1, keepdims=True)
    return (x * lax.rsqrt(ms + EPS)) * w


def _silu(x):
    return x * jax.nn.sigmoid(x)


def _dot(a, b):
    return jnp.dot(a, b, preferred_element_type=F32)


def _mixer_stage(cd_ref, x_ref, cos_ref, sin_ref, dmask_ref, qd_ref, kd_ref,
                 mix_pre_ref, w_in_ref, dw_w_ref, dw_b_ref, ln_w_ref, ln_b_ref,
                 u_ref, glu_ref, state_ref, mix_ref, xkeep_ref):
    ts = x_ref.shape[0]
    h = _rmsnorm(x_ref[...], mix_pre_ref[...]).astype(BF16)
    yield
    u_ref[:, 0:Q0] = _dot(h, w_in_ref[:, 0:Q0])
    yield
    glu_ref[0, HALO:HALO + ts, :] = u_ref[:, 0:D_CONV] * jax.nn.sigmoid(u_ref[:, D_CONV:2 * D_CONV])
    yield
    for b in range(1, SUBLANES):
        glu_ref[b, SUBLANES:HALO + ts, :] = glu_ref[0, SUBLANES - b:HALO + ts - b, :]
    yield
    groups = CONV_ROWS // SUBLANES
    for r0 in range(0, ts, CONV_ROWS):
        accs = [dw_b_ref[...]] * groups
        for dist in range(CONV_WIDTH):
            a, b = divmod(dist, SUBLANES)
            w_tap = dw_w_ref[CONV_WIDTH - 1 - dist]
            start = HALO + r0 - SUBLANES * a
            for i in range(groups):
                lo = start + SUBLANES * i
                accs[i] = accs[i] + glu_ref[b, lo:lo + SUBLANES, :] * w_tap
        acc = jnp.concatenate(accs, axis=0)
        mu = jnp.mean(acc, axis=-1, keepdims=True)
        cen = acc - mu
        var = jnp.mean(cen * cen, axis=-1, keepdims=True)
        yn = (cen * lax.rsqrt(var + EPS)) * ln_w_ref[...] + ln_b_ref[...]
        mix_ref[r0:r0 + CONV_ROWS, 0:D_CONV] = _silu(yn).astype(BF16)
        if r0 + CONV_ROWS == ts:
            glu_ref[0, 0:HALO, :] = glu_ref[0, ts:ts + HALO, :]
        yield

    u_ref[:, Q0:D_IN] = _dot(h, w_in_ref[:, Q0:D_IN])
    yield
    scale = HEAD_DIM ** -0.5
    units = [(c0, hd) for c0 in range(0, ts, CHUNK) for hd in range(N_HEADS)]
    qbs, vs, scores, carried = {}, {}, {}, {}
    for c0, hd in units:
        lo = hd * HEAD_DIM
        cosf = cos_ref[c0:c0 + CHUNK, :]
        sinf = sin_ref[c0:c0 + CHUNK, :]
        q = u_ref[c0:c0 + CHUNK, Q0 + lo:Q0 + lo + HEAD_DIM]
        k = u_ref[c0:c0 + CHUNK, K0 + lo:K0 + lo + HEAD_DIM]
        v = u_ref[c0:c0 + CHUNK, V0 + lo:V0 + lo + HEAD_DIM].astype(BF16)
        q = q * cosf + pltpu.roll(q, HEAD_DIM // 2, 1) * sinf
        k = (k * cosf + pltpu.roll(k, HEAD_DIM // 2, 1) * sinf) * scale
        qb = q.astype(BF16)
        scores[c0, hd] = lax.dot_general(qb, k.astype(BF16), (((1,), (1,)), ((), ())),
                                         preferred_element_type=F32)
        kdec = (k * kd_ref[:, lo:lo + HEAD_DIM]).astype(BF16)
        kv = lax.dot_general(kdec, v, (((0,), (0,)), ((), ())), preferred_element_type=F32)
        state = state_ref[hd]
        carried[c0, hd] = state.astype(BF16)
        state_ref[hd] = state * cd_ref[hd] + kv
        qbs[c0, hd], vs[c0, hd] = qb, v
    yield
    outs = {}
    for c0, hd in units:
        lo = hd * HEAD_DIM
        sc = (scores[c0, hd] * dmask_ref[hd]).astype(BF16)
        inner = _dot(sc, vs[c0, hd])
        cross = _dot(qbs[c0, hd], carried[c0, hd]) * qd_ref[:, lo:lo + HEAD_DIM]
        outs[c0, hd] = inner + cross
    yield
    for c0, hd in units:
        lo = hd * HEAD_DIM
        o = outs[c0, hd]
        o = o * lax.rsqrt(jnp.mean(o * o, axis=-1, keepdims=True) + EPS)
        g = u_ref[c0:c0 + CHUNK, G0 + lo:G0 + lo + HEAD_DIM]
        mix_ref[c0:c0 + CHUNK, D_CONV + lo:D_CONV + lo + HEAD_DIM] = (_silu(g) * o).astype(BF16)
    xkeep_ref[...] = x_ref[...]
    yield


def _channel_stage(mix_ref, xkeep_ref, p_ref, w_out_ref, mix_post_ref, ffn_pre_ref, w_gate_ref,
                   w_up_ref, w_down_ref, ffn_post_ref, w_pg_ref, w_pp_ref, ple_post_ref,
                   o_ref, act_ref):
    y = _dot(mix_ref[...], w_out_ref[...])
    yield
    x = xkeep_ref[...] + _rmsnorm(y, mix_post_ref[...])
    h = _rmsnorm(x, ffn_pre_ref[...]).astype(BF16)
    yield
    for c0 in range(0, D_FF, FFN_COLS):
        gate = _dot(h, w_gate_ref[:, c0:c0 + FFN_COLS])
        up = _dot(h, w_up_ref[:, c0:c0 + FFN_COLS])
        act_ref[:, c0:c0 + FFN_COLS] = (_silu(gate) * up).astype(BF16)
        yield
    parts = []
    for c0 in range(0, D_MODEL, DOWN_COLS):
        parts.append(_dot(act_ref[...], w_down_ref[:, c0:c0 + DOWN_COLS]))
        yield
    x = x + _rmsnorm(jnp.concatenate(parts, axis=1), ffn_post_ref[...])
    yield
    e_gate = _dot(x.astype(BF16), w_pg_ref[...])
    e_proj = _dot(p_ref[...].astype(BF16), w_pp_ref[...])
    yield
    e = jax.nn.sigmoid(e_gate) * e_proj
    o_ref[...] = x + _rmsnorm(e, ple_post_ref[...])
    yield


def _block_kernel(tiles_per_row,
                  cd_ref, x_ref, p_ref, cos_ref, sin_ref, dmask_ref, qd_ref, kd_ref,
                  mix_pre_ref, w_in_ref, dw_w_ref, dw_b_ref, ln_w_ref, ln_b_ref,
                  w_out_ref, mix_post_ref, ffn_pre_ref, w_gate_ref, w_up_ref,
                  w_down_ref, ffn_post_ref, w_pg_ref, w_pp_ref, ple_post_ref,
                  o_ref,
                  u_ref, glu_ref, mix_ref, xkeep_ref, act_ref, state_ref):
    n = pl.program_id(0)

    @pl.when(n == 0)
    def _():
        mix_ref[...] = jnp.zeros_like(mix_ref)
        xkeep_ref[...] = jnp.zeros_like(xkeep_ref)

    @pl.when(n % tiles_per_row == 0)
    def _():
        state_ref[...] = jnp.zeros_like(state_ref)
        glu_ref[0, 0:HALO, :] = jnp.zeros((HALO, D_CONV), F32)

    stages = {
        "C": _channel_stage(mix_ref, xkeep_ref, p_ref, w_out_ref, mix_post_ref, ffn_pre_ref,
                            w_gate_ref, w_up_ref, w_down_ref, ffn_post_ref, w_pg_ref, w_pp_ref,
                            ple_post_ref, o_ref, act_ref),
        "M": _mixer_stage(cd_ref, x_ref, cos_ref, sin_ref, dmask_ref, qd_ref, kd_ref,
                          mix_pre_ref, w_in_ref, dw_w_ref, dw_b_ref, ln_w_ref, ln_b_ref,
                          u_ref, glu_ref, state_ref, mix_ref, xkeep_ref),
    }
    for key in SEGMENT_ORDER.replace(" ", ""):
        next(stages[key])
    for stage in stages.values():
        assert next(stage, _EXHAUSTED) is _EXHAUSTED, "SEGMENT_ORDER leaves a stage unfinished"


def _retention_tables(seq):
    half = HEAD_DIM // 2
    pos = jnp.arange(seq, dtype=F32)
    freqs = ROPE_BASE ** (-jnp.arange(half, dtype=F32) / half)
    ang = pos[:, None] * freqs[None, :]
    cos, sin = jnp.cos(ang), jnp.sin(ang)
    cosf = jnp.concatenate([cos, cos], axis=-1)
    sinf = jnp.concatenate([-sin, sin], axis=-1)
    log_g = jnp.log(1.0 - 2.0 ** (-5.0 - jnp.arange(N_HEADS, dtype=F32)))
    idx = jnp.arange(CHUNK, dtype=F32)
    rel = idx[:, None] - idx[None, :]
    dmask = jnp.where(rel >= 0, jnp.exp(log_g[:, None, None] * jnp.maximum(rel, 0.0)), 0.0)
    q_decay = jnp.exp(log_g[:, None] * (idx + 1.0))
    k_decay = jnp.exp(log_g[:, None] * (CHUNK - 1.0 - idx))
    chunk_decay = jnp.exp(log_g * CHUNK)
    widen = lambda t: jnp.repeat(t.T, HEAD_DIM, axis=1)
    return cosf, sinf, dmask, widen(q_decay), widen(k_decay), chunk_decay


def kernel(x, p, mix_pre_norm, w_in, conv_dw_w, conv_dw_b, conv_ln_w, conv_ln_b, w_out,
           mix_post_norm, ffn_pre_norm, w_ffn_gate, w_ffn_up, w_ffn_down, ffn_post_norm,
           w_ple_gate, w_ple_proj, ple_post_norm):
    batch, seq, d_model = x.shape
    depth = p.shape[0]
    assert depth == 1 and d_model == D_MODEL and seq % SEQ_TILE == 0 and SEQ_TILE % CHUNK == 0
    ts = SEQ_TILE
    tiles_per_row = seq // ts
    n_tiles = batch * tiles_per_row
    cosf, sinf, dmask, qd, kd, cd = _retention_tables(seq)

    row = lambda v: v.reshape(1, -1)
    wb = lambda w: w[0].astype(BF16)

    mixer_tile = lambda n: jnp.minimum(n, n_tiles - 1)
    channel_tile = lambda n: jnp.maximum(n - 1, 0)

    def const_spec(shape):
        zeros = (0,) * len(shape)
        return pl.BlockSpec(shape, lambda n: zeros, pipeline_mode=pl.Buffered(1))

    rope_spec = pl.BlockSpec((ts, HEAD_DIM), lambda n: (mixer_tile(n) % tiles_per_row, 0))
    operands = [
        (cd, pl.BlockSpec(memory_space=pltpu.SMEM)),
        (x.reshape(batch * seq, D_MODEL), pl.BlockSpec((ts, D_MODEL), lambda n: (mixer_tile(n), 0))),
        (p.reshape(batch * seq, D_PLE), pl.BlockSpec((ts, D_PLE), lambda n: (channel_tile(n), 0))),
        (cosf, rope_spec),
        (sinf, rope_spec),
        (dmask, const_spec((N_HEADS, CHUNK, CHUNK))),
        (qd, const_spec((CHUNK, D_RET))),
        (kd, const_spec((CHUNK, D_RET))),
        (row(mix_pre_norm[0]), const_spec((1, D_MODEL))),
        (wb(w_in), const_spec((D_MODEL, D_IN))),
        (jnp.broadcast_to(conv_dw_w[0][:, None, :], (CONV_WIDTH, SUBLANES, D_CONV)),
         const_spec((CONV_WIDTH, SUBLANES, D_CONV))),
        (jnp.broadcast_to(conv_dw_b[0][None, :], (SUBLANES, D_CONV)), const_spec((SUBLANES, D_CONV))),
        (row(conv_ln_w[0]), const_spec((1, D_CONV))),
        (row(conv_ln_b[0]), const_spec((1, D_CONV))),
        (wb(w_out), const_spec((D_MODEL, D_MODEL))),
        (row(mix_post_norm[0]), const_spec((1, D_MODEL))),
        (row(ffn_pre_norm[0]), const_spec((1, D_MODEL))),
        (wb(w_ffn_gate), const_spec((D_MODEL, D_FF))),
        (wb(w_ffn_up), const_spec((D_MODEL, D_FF))),
        (wb(w_ffn_down), const_spec((D_FF, D_MODEL))),
        (row(ffn_post_norm[0]), const_spec((1, D_MODEL))),
        (wb(w_ple_gate), const_spec((D_MODEL, D_MODEL))),
        (wb(w_ple_proj), const_spec((D_PLE, D_MODEL))),
        (row(ple_post_norm[0]), const_spec((1, D_MODEL))),
    ]
    args = [a for a, _ in operands]
    in_specs = [s for _, s in operands]

    out = pl.pallas_call(
        functools.partial(_block_kernel, tiles_per_row),
        grid=(n_tiles + 1,),
        in_specs=in_specs,
        out_specs=pl.BlockSpec((ts, D_MODEL), lambda n: (channel_tile(n), 0)),
        out_shape=jax.ShapeDtypeStruct((batch * seq, D_MODEL), x.dtype),
        scratch_shapes=[
            pltpu.VMEM((ts, D_IN), F32),
            pltpu.VMEM((SUBLANES, HALO + ts, D_CONV), F32),
            pltpu.VMEM((ts, D_MODEL), BF16),
            pltpu.VMEM((ts, D_MODEL), F32),
            pltpu.VMEM((ts, D_FF), BF16),
            pltpu.VMEM((N_HEADS, HEAD_DIM, HEAD_DIM), F32),
        ],
        compiler_params=pltpu.CompilerParams(
            dimension_semantics=("arbitrary",),
            vmem_limit_bytes=VMEM_LIMIT_BYTES,
        ),
        name="hymba_block",
    )(*args)
    return out.reshape(batch, seq, D_MODEL)
```

```python
import functools

import jax
import jax.numpy as jnp
from jax import lax
from jax.experimental import pallas as pl
from jax.experimental.pallas import tpu as pltpu

D_MODEL = 1024
D_CONV = 512
D_RET = 512
N_HEADS = 4
HEAD_DIM = 128
CONV_WIDTH = 31
CHUNK = 128
D_FF = 2816
D_PLE = 256
D_IN = 2 * D_CONV + 4 * D_RET
ROPE_BASE = 10000.0
EPS = 1e-6

SUBLANES = 8
LANES = 128
SEQ_TILE = 256
HALO = 32
CONV_ROWS = 32
FFN_COLS = 256
DOWN_COLS = 256
VMEM_LIMIT_BYTES = 56 * 1024 * 1024

SEGMENT_ORDER = (
    "CMMC "
    "CMCMCMCMCMCMCMCMCMCMC "
    "CCCC "
    "MC "
    "CMMMC"
)

Q0 = 2 * D_CONV
K0 = Q0 + D_RET
V0 = K0 + D_RET
G0 = V0 + D_RET

F32 = jnp.float32
BF16 = jnp.bfloat16
_EXHAUSTED = object()


def _rmsnorm(x, w):
    ms = jnp.mean(x * x, axis=-1, keepdims=True)
    return (x * lax.rsqrt(ms + EPS)) * w


def _silu(x):
    return x * jax.nn.sigmoid(x)


def _dot(a, b):
    return jnp.dot(a, b, preferred_element_type=F32)


def _mixer_stage(cd_ref, x_ref, cos_ref, sin_ref, dmask_ref, qd_ref, kd_ref,
                 mix_pre_ref, w_in_ref, dw_w_ref, dw_b_ref, ln_w_ref, ln_b_ref,
                 u_ref, glu_ref, state_ref, mix_ref):
    ts = x_ref.shape[0]
    h = _rmsnorm(x_ref[...], mix_pre_ref[...]).astype(BF16)
    yield
    u_ref[:, 0:Q0] = _dot(h, w_in_ref[:, 0:Q0])
    yield
    glu_ref[0, HALO:HALO + ts, :] = u_ref[:, 0:D_CONV] * jax.nn.sigmoid(u_ref[:, D_CONV:2 * D_CONV])
    yield
    for b in range(1, SUBLANES):
        glu_ref[b, SUBLANES:HALO + ts, :] = glu_ref[0, SUBLANES - b:HALO + ts - b, :]
    yield
    groups = CONV_ROWS // SUBLANES
    for r0 in range(0, ts, CONV_ROWS):
        accs = [dw_b_ref[...]] * groups
        for b in range(SUBLANES):
            n_a = (CONV_WIDTH - 1 - b) // SUBLANES + 1
            base = HALO + r0 - SUBLANES * (n_a - 1)
            w_taps = [dw_w_ref[CONV_WIDTH - 1 - (SUBLANES * a + b)] for a in range(n_a)]
            for j in range(groups + n_a - 1):
                row = glu_ref[b, base + SUBLANES * j:base + SUBLANES * (j + 1), :]
                for a in range(n_a):
                    i = j - (n_a - 1) + a
                    if 0 <= i < groups:
                        accs[i] = accs[i] + row * w_taps[a]
        acc = jnp.concatenate(accs, axis=0)
        mu = jnp.mean(acc, axis=-1, keepdims=True)
        cen = acc - mu
        var = jnp.mean(cen * cen, axis=-1, keepdims=True)
        yn = (cen * lax.rsqrt(var + EPS)) * ln_w_ref[...] + ln_b_ref[...]
        mix_ref[r0:r0 + CONV_ROWS, 0:D_CONV] = _silu(yn).astype(BF16)
        if r0 + CONV_ROWS == ts:
            glu_ref[0, 0:HALO, :] = glu_ref[0, ts:ts + HALO, :]
        yield

    u_ref[:, Q0:D_IN] = _dot(h, w_in_ref[:, Q0:D_IN])
    yield
    scale = HEAD_DIM ** -0.5
    units = [(c0, hd) for c0 in range(0, ts, CHUNK) for hd in range(N_HEADS)]
    qbs, vs, scores, carried = {}, {}, {}, {}
    for c0, hd in units:
        lo = hd * HEAD_DIM
        cosf = cos_ref[c0:c0 + CHUNK, :]
        sinf = sin_ref[c0:c0 + CHUNK, :]
        q = u_ref[c0:c0 + CHUNK, Q0 + lo:Q0 + lo + HEAD_DIM]
        k = u_ref[c0:c0 + CHUNK, K0 + lo:K0 + lo + HEAD_DIM]
        v = u_ref[c0:c0 + CHUNK, V0 + lo:V0 + lo + HEAD_DIM].astype(BF16)
        q = q * cosf + pltpu.roll(q, HEAD_DIM // 2, 1) * sinf
        k = (k * cosf + pltpu.roll(k, HEAD_DIM // 2, 1) * sinf) * scale
        qb = q.astype(BF16)
        scores[c0, hd] = lax.dot_general(qb, k.astype(BF16), (((1,), (1,)), ((), ())),
                                         preferred_element_type=F32)
        kdec = (k * kd_ref[:, lo:lo + HEAD_DIM]).astype(BF16)
        kv = lax.dot_general(kdec, v, (((0,), (0,)), ((), ())), preferred_element_type=F32)
        state = state_ref[hd]
        carried[c0, hd] = state.astype(BF16)
        state_ref[hd] = state * cd_ref[hd] + kv
        qbs[c0, hd], vs[c0, hd] = qb, v
    yield
    outs = {}
    for c0, hd in units:
        lo = hd * HEAD_DIM
        sc = (scores[c0, hd] * dmask_ref[hd]).astype(BF16)
        inner = _dot(sc, vs[c0, hd])
        cross = _dot(qbs[c0, hd], carried[c0, hd]) * qd_ref[:, lo:lo + HEAD_DIM]
        outs[c0, hd] = inner + cross
    yield
    for c0, hd in units:
        lo = hd * HEAD_DIM
        o = outs[c0, hd]
        o = o * lax.rsqrt(jnp.mean(o * o, axis=-1, keepdims=True) + EPS)
        g = u_ref[c0:c0 + CHUNK, G0 + lo:G0 + lo + HEAD_DIM]
        mix_ref[c0:c0 + CHUNK, D_CONV + lo:D_CONV + lo + HEAD_DIM] = (_silu(g) * o).astype(BF16)
    yield


def _channel_stage(mix_ref, xc_ref, p_ref, w_out_ref, mix_post_ref, ffn_pre_ref, w_gate_ref,
                   w_up_ref, w_down_ref, ffn_post_ref, w_pg_ref, w_pp_ref, ple_post_ref,
                   o_ref, act_ref):
    y = _dot(mix_ref[...], w_out_ref[:, 0:D_MODEL])
    yield
    x = xc_ref[...] + _rmsnorm(y, mix_post_ref[...])
    h = _rmsnorm(x, ffn_pre_ref[...]).astype(BF16)
    yield
    for c0 in range(0, D_FF, FFN_COLS):
        gate = _dot(h, w_gate_ref[:, c0:c0 + FFN_COLS])
        up = _dot(h, w_up_ref[:, c0:c0 + FFN_COLS])
        act_ref[:, c0:c0 + FFN_COLS] = (_silu(gate) * up).astype(BF16)
        yield
    parts = []
    for c0 in range(0, D_MODEL, DOWN_COLS):
        parts.append(_dot(act_ref[...], w_down_ref[:, c0:c0 + DOWN_COLS]))
        yield
    x = x + _rmsnorm(jnp.concatenate(parts, axis=1), ffn_post_ref[...])
    yield
    e_gate = _dot(x.astype(BF16), w_pg_ref[:, 0:D_MODEL])
    e_proj = _dot(p_ref[...].astype(BF16), w_pp_ref[:, 0:D_MODEL])
    yield
    e = jax.nn.sigmoid(e_gate) * e_proj
    o_ref[...] = x + _rmsnorm(e, ple_post_ref[...])
    yield


def _block_kernel(tiles_per_row,
                  cd_ref, x_ref, xc_ref, p_ref, cos_ref, sin_ref, dmask_ref, qd_ref, kd_ref,
                  mix_pre_ref, w_in_ref, dw_w_ref, dw_b_ref, ln_w_ref, ln_b_ref,
                  w_out_ref, mix_post_ref, ffn_pre_ref, w_gate_ref, w_up_ref,
                  w_down_ref, ffn_post_ref, w_pg_ref, w_pp_ref, ple_post_ref,
                  o_ref,
                  u_ref, glu_ref, mix_ref, act_ref, state_ref):
    n = pl.program_id(0)

    @pl.when(n == 0)
    def _():
        mix_ref[...] = jnp.zeros_like(mix_ref)

    @pl.when(n % tiles_per_row == 0)
    def _():
        state_ref[...] = jnp.zeros_like(state_ref)
        glu_ref[0, 0:HALO, :] = jnp.zeros((HALO, D_CONV), F32)

    stages = {
        "C": _channel_stage(mix_ref, xc_ref, p_ref, w_out_ref, mix_post_ref, ffn_pre_ref,
                            w_gate_ref, w_up_ref, w_down_ref, ffn_post_ref, w_pg_ref, w_pp_ref,
                            ple_post_ref, o_ref, act_ref),
        "M": _mixer_stage(cd_ref, x_ref, cos_ref, sin_ref, dmask_ref, qd_ref, kd_ref,
                          mix_pre_ref, w_in_ref, dw_w_ref, dw_b_ref, ln_w_ref, ln_b_ref,
                          u_ref, glu_ref, state_ref, mix_ref),
    }
    for key in SEGMENT_ORDER.replace(" ", ""):
        next(stages[key])
    for stage in stages.values():
        assert next(stage, _EXHAUSTED) is _EXHAUSTED, "SEGMENT_ORDER leaves a stage unfinished"


def _retention_tables(seq):
    half = HEAD_DIM // 2
    pos = jnp.arange(seq, dtype=F32)
    freqs = ROPE_BASE ** (-jnp.arange(half, dtype=F32) / half)
    ang = pos[:, None] * freqs[None, :]
    cos, sin = jnp.cos(ang), jnp.sin(ang)
    cosf = jnp.concatenate([cos, cos], axis=-1)
    sinf = jnp.concatenate([-sin, sin], axis=-1)
    log_g = jnp.log(1.0 - 2.0 ** (-5.0 - jnp.arange(N_HEADS, dtype=F32)))
    idx = jnp.arange(CHUNK, dtype=F32)
    rel = idx[:, None] - idx[None, :]
    dmask = jnp.where(rel >= 0, jnp.exp(log_g[:, None, None] * jnp.maximum(rel, 0.0)), 0.0)
    q_decay = jnp.exp(log_g[:, None] * (idx + 1.0))
    k_decay = jnp.exp(log_g[:, None] * (CHUNK - 1.0 - idx))
    chunk_decay = jnp.exp(log_g * CHUNK)
    widen = lambda t: jnp.repeat(t.T, HEAD_DIM, axis=1)
    return cosf, sinf, dmask, widen(q_decay), widen(k_decay), chunk_decay


def kernel(x, p, mix_pre_norm, w_in, conv_dw_w, conv_dw_b, conv_ln_w, conv_ln_b, w_out,
           mix_post_norm, ffn_pre_norm, w_ffn_gate, w_ffn_up, w_ffn_down, ffn_post_norm,
           w_ple_gate, w_ple_proj, ple_post_norm):
    batch, seq, d_model = x.shape
    depth = p.shape[0]
    assert depth == 1 and d_model == D_MODEL and seq % SEQ_TILE == 0 and SEQ_TILE % CHUNK == 0
    ts = SEQ_TILE
    tiles_per_row = seq // ts
    n_tiles = batch * tiles_per_row
    cosf, sinf, dmask, qd, kd, cd = _retention_tables(seq)

    row = lambda v: v.reshape(1, -1)
    wb = lambda w: w[0].astype(BF16)

    def wb_odd_pitch(w):
        w = wb(w)
        spare = LANES if (w.shape[1] // LANES) % SUBLANES == 0 else 0
        return jnp.pad(w, ((0, 0), (0, spare)))

    padded = lambda cols: cols + (LANES if (cols // LANES) % SUBLANES == 0 else 0)

    mixer_tile = lambda n: jnp.minimum(n, n_tiles - 1)
    channel_tile = lambda n: jnp.maximum(n - 1, 0)

    def const_spec(shape):
        zeros = (0,) * len(shape)
        return pl.BlockSpec(shape, lambda n: zeros, pipeline_mode=pl.Buffered(1))

    rope_spec = pl.BlockSpec((ts, HEAD_DIM), lambda n: (mixer_tile(n) % tiles_per_row, 0))
    x_flat = x.reshape(batch * seq, D_MODEL)
    operands = [
        (cd, pl.BlockSpec(memory_space=pltpu.SMEM)),
        (x_flat, pl.BlockSpec((ts, D_MODEL), lambda n: (mixer_tile(n), 0))),
        (x_flat, pl.BlockSpec((ts, D_MODEL), lambda n: (channel_tile(n), 0))),
        (p.reshape(batch * seq, D_PLE), pl.BlockSpec((ts, D_PLE), lambda n: (channel_tile(n), 0))),
        (cosf, rope_spec),
        (sinf, rope_spec),
        (dmask, const_spec((N_HEADS, CHUNK, CHUNK))),
        (qd, const_spec((CHUNK, D_RET))),
        (kd, const_spec((CHUNK, D_RET))),
        (row(mix_pre_norm[0]), const_spec((1, D_MODEL))),
        (wb_odd_pitch(w_in), const_spec((D_MODEL, padded(D_IN)))),
        (jnp.broadcast_to(conv_dw_w[0][:, None, :], (CONV_WIDTH, SUBLANES, D_CONV)),
         const_spec((CONV_WIDTH, SUBLANES, D_CONV))),
        (jnp.broadcast_to(conv_dw_b[0][None, :], (SUBLANES, D_CONV)), const_spec((SUBLANES, D_CONV))),
        (row(conv_ln_w[0]), const_spec((1, D_CONV))),
        (row(conv_ln_b[0]), const_spec((1, D_CONV))),
        (wb_odd_pitch(w_out), const_spec((D_MODEL, padded(D_MODEL)))),
        (row(mix_post_norm[0]), const_spec((1, D_MODEL))),
        (row(ffn_pre_norm[0]), const_spec((1, D_MODEL))),
        (wb(w_ffn_gate), const_spec((D_MODEL, D_FF))),
        (wb(w_ffn_up), const_spec((D_MODEL, D_FF))),
        (wb_odd_pitch(w_ffn_down), const_spec((D_FF, padded(D_MODEL)))),
        (row(ffn_post_norm[0]), const_spec((1, D_MODEL))),
        (wb_odd_pitch(w_ple_gate), const_spec((D_MODEL, padded(D_MODEL)))),
        (wb_odd_pitch(w_ple_proj), const_spec((D_PLE, padded(D_MODEL)))),
        (row(ple_post_norm[0]), const_spec((1, D_MODEL))),
    ]
    args = [a for a, _ in operands]
    in_specs = [s for _, s in operands]

    out = pl.pallas_call(
        functools.partial(_block_kernel, tiles_per_row),
        grid=(n_tiles + 1,),
        in_specs=in_specs,
        out_specs=pl.BlockSpec((ts, D_MODEL), lambda n: (channel_tile(n), 0)),
        out_shape=jax.ShapeDtypeStruct((batch * seq, D_MODEL), x.dtype),
        scratch_shapes=[
            pltpu.VMEM((ts, D_IN), F32),
            pltpu.VMEM((SUBLANES, HALO + ts, D_CONV), F32),
            pltpu.VMEM((ts, D_MODEL), BF16),
            pltpu.VMEM((ts, D_FF), BF16),
            pltpu.VMEM((N_HEADS, HEAD_DIM, HEAD_DIM), F32),
        ],
        compiler_params=pltpu.CompilerParams(
            dimension_semantics=("arbitrary",),
            vmem_limit_bytes=VMEM_LIMIT_BYTES,
        ),
        name="hymba_block",
    )(*args)
    return out.reshape(batch, seq, D_MODEL)
```

```python
import functools

import jax
import jax.numpy as jnp
import numpy as np
from jax import lax
from jax.experimental import pallas as pl
from jax.experimental.pallas import tpu as pltpu

D_MODEL = 1024
D_CONV = 512
D_RET = 512
N_HEADS = 4
HEAD_DIM = 128
CONV_WIDTH = 31
CHUNK = 128
D_FF = 2816
D_PLE = 256
D_IN = 2 * D_CONV + 4 * D_RET
ROPE_BASE = 10000.0
EPS = 1e-6

SUBLANES = 8
LANES = 128
SEQ_TILE = 256
HALO = 32
CONV_ROWS = 64
FFN_COLS = 256
DOWN_COLS = 256
VMEM_LIMIT_BYTES = 56 * 1024 * 1024

SEGMENT_ORDER = (
    "CMMC "
    "CMCMCMCMCMCMCCCCC "
    "CCCC "
    "MC "
    "CMMMC"
)

Q0 = 2 * D_CONV
K0 = Q0 + D_RET
V0 = K0 + D_RET
G0 = V0 + D_RET

F32 = jnp.float32
BF16 = jnp.bfloat16
_EXHAUSTED = object()


def _rmsnorm(x, w):
    ms = jnp.mean(x * x, axis=-1, keepdims=True)
    return (x * lax.rsqrt(ms + EPS)) * w


def _silu(x):
    return x * jax.nn.sigmoid(x)


def _dot(a, b):
    return jnp.dot(a, b, preferred_element_type=F32)


def _mixer_stage(cd_ref, x_ref, cos_ref, sin_ref, dmask_ref, qd_ref, kd_ref,
                 mix_pre_ref, w_in_ref, dw_w_ref, dw_b_ref, ln_w_ref, ln_b_ref,
                 u_ref, glu_ref, state_ref, mix_ref):
    ts = x_ref.shape[0]
    h = _rmsnorm(x_ref[...], mix_pre_ref[...]).astype(BF16)
    yield
    u_ref[:, 0:Q0] = _dot(h, w_in_ref[:, 0:Q0])
    yield
    glu_ref[0, HALO:HALO + ts, :] = u_ref[:, 0:D_CONV] * jax.nn.sigmoid(u_ref[:, D_CONV:2 * D_CONV])
    yield
    for b in range(1, SUBLANES):
        glu_ref[b, SUBLANES:HALO + ts, :] = glu_ref[0, SUBLANES - b:HALO + ts - b, :]
    yield
    groups = CONV_ROWS // SUBLANES
    for r0 in range(0, ts, CONV_ROWS):
        accs = [dw_b_ref[...]] * groups
        for b in range(SUBLANES):
            n_a = (CONV_WIDTH - 1 - b) // SUBLANES + 1
            base = HALO + r0 - SUBLANES * (n_a - 1)
            w_taps = [dw_w_ref[CONV_WIDTH - 1 - (SUBLANES * a + b)] for a in range(n_a)]
            for j in range(groups + n_a - 1):
                row = glu_ref[b, base + SUBLANES * j:base + SUBLANES * (j + 1), :]
                for a in range(n_a):
                    i = j - (n_a - 1) + a
                    if 0 <= i < groups:
                        accs[i] = accs[i] + row * w_taps[a]
        acc = jnp.concatenate(accs, axis=0)
        mu = jnp.mean(acc, axis=-1, keepdims=True)
        cen = acc - mu
        var = jnp.mean(cen * cen, axis=-1, keepdims=True)
        yn = (cen * lax.rsqrt(var + EPS)) * ln_w_ref[...] + ln_b_ref[...]
        mix_ref[r0:r0 + CONV_ROWS, 0:D_CONV] = _silu(yn).astype(BF16)
        if r0 + CONV_ROWS == ts:
            glu_ref[0, 0:HALO, :] = glu_ref[0, ts:ts + HALO, :]
        yield

    u_ref[:, Q0:D_IN] = _dot(h, w_in_ref[:, Q0:D_IN])
    yield
    scale = HEAD_DIM ** -0.5
    units = [(c0, hd) for c0 in range(0, ts, CHUNK) for hd in range(N_HEADS)]
    qbs, vs, scores, carried = {}, {}, {}, {}
    for c0, hd in units:
        lo = hd * HEAD_DIM
        cosf = cos_ref[c0:c0 + CHUNK, :]
        sinf = sin_ref[c0:c0 + CHUNK, :]
        q = u_ref[c0:c0 + CHUNK, Q0 + lo:Q0 + lo + HEAD_DIM]
        k = u_ref[c0:c0 + CHUNK, K0 + lo:K0 + lo + HEAD_DIM]
        v = u_ref[c0:c0 + CHUNK, V0 + lo:V0 + lo + HEAD_DIM].astype(BF16)
        q = q * cosf + pltpu.roll(q, HEAD_DIM // 2, 1) * sinf
        k = (k * cosf + pltpu.roll(k, HEAD_DIM // 2, 1) * sinf) * scale
        qb = q.astype(BF16)
        scores[c0, hd] = lax.dot_general(qb, k.astype(BF16), (((1,), (1,)), ((), ())),
                                         preferred_element_type=F32)
        kdec = (k * kd_ref[:, lo:lo + HEAD_DIM]).astype(BF16)
        kv = lax.dot_general(kdec, v, (((0,), (0,)), ((), ())), preferred_element_type=F32)
        state = state_ref[hd]
        carried[c0, hd] = state.astype(BF16)
        state_ref[hd] = state * cd_ref[hd] + kv
        qbs[c0, hd], vs[c0, hd] = qb, v
    yield
    outs = {}
    for c0, hd in units:
        lo = hd * HEAD_DIM
        sc = (scores[c0, hd] * dmask_ref[hd]).astype(BF16)
        inner = _dot(sc, vs[c0, hd])
        cross = _dot(qbs[c0, hd], carried[c0, hd]) * qd_ref[:, lo:lo + HEAD_DIM]
        outs[c0, hd] = inner + cross
    yield
    for c0, hd in units:
        lo = hd * HEAD_DIM
        o = outs[c0, hd]
        o = o * lax.rsqrt(jnp.mean(o * o, axis=-1, keepdims=True) + EPS)
        g = u_ref[c0:c0 + CHUNK, G0 + lo:G0 + lo + HEAD_DIM]
        mix_ref[c0:c0 + CHUNK, D_CONV + lo:D_CONV + lo + HEAD_DIM] = (_silu(g) * o).astype(BF16)
    yield


def _channel_stage(mix_ref, xc_ref, p_ref, w_out_ref, mix_post_ref, ffn_pre_ref, w_gate_ref,
                   w_up_ref, w_down_ref, ffn_post_ref, w_pg_ref, w_pp_ref, ple_post_ref,
                   o_ref, act_ref):
    y = _dot(mix_ref[...], w_out_ref[:, 0:D_MODEL])
    yield
    x = xc_ref[...] + _rmsnorm(y, mix_post_ref[...])
    h = _rmsnorm(x, ffn_pre_ref[...]).astype(BF16)
    yield
    for c0 in range(0, D_FF, FFN_COLS):
        gate = _dot(h, w_gate_ref[:, c0:c0 + FFN_COLS])
        up = _dot(h, w_up_ref[:, c0:c0 + FFN_COLS])
        act_ref[:, c0:c0 + FFN_COLS] = (_silu(gate) * up).astype(BF16)
        yield
    parts = []
    for c0 in range(0, D_MODEL, DOWN_COLS):
        parts.append(_dot(act_ref[...], w_down_ref[:, c0:c0 + DOWN_COLS]))
        yield
    x = x + _rmsnorm(jnp.concatenate(parts, axis=1), ffn_post_ref[...])
    yield
    e_gate = _dot(x.astype(BF16), w_pg_ref[:, 0:D_MODEL])
    e_proj = _dot(p_ref[...].astype(BF16), w_pp_ref[:, 0:D_MODEL])
    yield
    e = jax.nn.sigmoid(e_gate) * e_proj
    o_ref[...] = x + _rmsnorm(e, ple_post_ref[...])
    yield


def _block_kernel(tiles_per_row,
                  cd_ref, x_ref, xc_ref, p_ref, cos_ref, sin_ref, dmask_ref, qd_ref, kd_ref,
                  mix_pre_ref, w_in_ref, dw_w_ref, dw_b_ref, ln_w_ref, ln_b_ref,
                  w_out_ref, mix_post_ref, ffn_pre_ref, w_gate_ref, w_up_ref,
                  w_down_ref, ffn_post_ref, w_pg_ref, w_pp_ref, ple_post_ref,
                  o_ref,
                  u_ref, glu_ref, mix_ref, act_ref, state_ref):
    n = pl.program_id(0)

    @pl.when(n == 0)
    def _():
        mix_ref[...] = jnp.zeros_like(mix_ref)

    @pl.when(n % tiles_per_row == 0)
    def _():
        state_ref[...] = jnp.zeros_like(state_ref)
        glu_ref[0, 0:HALO, :] = jnp.zeros((HALO, D_CONV), F32)

    stages = {
        "C": _channel_stage(mix_ref, xc_ref, p_ref, w_out_ref, mix_post_ref, ffn_pre_ref,
                            w_gate_ref, w_up_ref, w_down_ref, ffn_post_ref, w_pg_ref, w_pp_ref,
                            ple_post_ref, o_ref, act_ref),
        "M": _mixer_stage(cd_ref, x_ref, cos_ref, sin_ref, dmask_ref, qd_ref, kd_ref,
                          mix_pre_ref, w_in_ref, dw_w_ref, dw_b_ref, ln_w_ref, ln_b_ref,
                          u_ref, glu_ref, state_ref, mix_ref),
    }
    for key in SEGMENT_ORDER.replace(" ", ""):
        next(stages[key])
    for stage in stages.values():
        assert next(stage, _EXHAUSTED) is _EXHAUSTED, "SEGMENT_ORDER leaves a stage unfinished"


def _retention_tables(seq):
    f32 = np.float32
    half = HEAD_DIM // 2
    pos = np.arange(seq, dtype=f32)
    freqs = (f32(ROPE_BASE) ** (-np.arange(half, dtype=f32) / f32(half))).astype(f32)
    ang = (pos[:, None] * freqs[None, :]).astype(f32)
    cos, sin = np.cos(ang).astype(f32), np.sin(ang).astype(f32)
    cosf = np.concatenate([cos, cos], axis=-1)
    sinf = np.concatenate([-sin, sin], axis=-1)
    log_g = np.log(f32(1.0) - f32(2.0) ** (f32(-5.0) - np.arange(N_HEADS, dtype=f32))).astype(f32)
    idx = np.arange(CHUNK, dtype=f32)
    rel = idx[:, None] - idx[None, :]
    dmask = np.where(rel >= 0, np.exp(log_g[:, None, None] * np.maximum(rel, f32(0.0))), f32(0.0)).astype(f32)
    q_decay = np.exp(log_g[:, None] * (idx + f32(1.0))).astype(f32)
    k_decay = np.exp(log_g[:, None] * (f32(CHUNK - 1.0) - idx)).astype(f32)
    chunk_decay = np.exp(log_g * f32(CHUNK)).astype(f32)
    widen = lambda t: np.repeat(t.T, HEAD_DIM, axis=1)
    tables = (cosf, sinf, dmask, widen(q_decay), widen(k_decay), chunk_decay)
    return tuple(jnp.asarray(t) for t in tables)


def kernel(x, p, mix_pre_norm, w_in, conv_dw_w, conv_dw_b, conv_ln_w, conv_ln_b, w_out,
           mix_post_norm, ffn_pre_norm, w_ffn_gate, w_ffn_up, w_ffn_down, ffn_post_norm,
           w_ple_gate, w_ple_proj, ple_post_norm):
    batch, seq, d_model = x.shape
    depth = p.shape[0]
    assert depth == 1 and d_model == D_MODEL and seq % SEQ_TILE == 0 and SEQ_TILE % CHUNK == 0
    ts = SEQ_TILE
    tiles_per_row = seq // ts
    n_tiles = batch * tiles_per_row
    cosf, sinf, dmask, qd, kd, cd = _retention_tables(seq)

    row = lambda v: v.reshape(1, -1)
    wb = lambda w: w[0].astype(BF16)

    def wb_odd_pitch(w):
        w = wb(w)
        spare = LANES if (w.shape[1] // LANES) % SUBLANES == 0 else 0
        return jnp.pad(w, ((0, 0), (0, spare)))

    padded = lambda cols: cols + (LANES if (cols // LANES) % SUBLANES == 0 else 0)

    mixer_tile = lambda n: jnp.minimum(n, n_tiles - 1)
    channel_tile = lambda n: jnp.maximum(n - 1, 0)

    def const_spec(shape):
        zeros = (0,) * len(shape)
        return pl.BlockSpec(shape, lambda n: zeros, pipeline_mode=pl.Buffered(1))

    rope_spec = pl.BlockSpec((ts, HEAD_DIM), lambda n: (mixer_tile(n) % tiles_per_row, 0))
    x_flat = x.reshape(batch * seq, D_MODEL)
    operands = [
        (cd, pl.BlockSpec(memory_space=pltpu.SMEM)),
        (x_flat, pl.BlockSpec((ts, D_MODEL), lambda n: (mixer_tile(n), 0))),
        (x_flat, pl.BlockSpec((ts, D_MODEL), lambda n: (channel_tile(n), 0))),
        (p.reshape(batch * seq, D_PLE), pl.BlockSpec((ts, D_PLE), lambda n: (channel_tile(n), 0))),
        (cosf, rope_spec),
        (sinf, rope_spec),
        (dmask, const_spec((N_HEADS, CHUNK, CHUNK))),
        (qd, const_spec((CHUNK, D_RET))),
        (kd, const_spec((CHUNK, D_RET))),
        (row(mix_pre_norm[0]), const_spec((1, D_MODEL))),
        (wb_odd_pitch(w_in), const_spec((D_MODEL, padded(D_IN)))),
        (jnp.broadcast_to(conv_dw_w[0][:, None, :], (CONV_WIDTH, SUBLANES, D_CONV)),
         const_spec((CONV_WIDTH, SUBLANES, D_CONV))),
        (jnp.broadcast_to(conv_dw_b[0][None, :], (SUBLANES, D_CONV)), const_spec((SUBLANES, D_CONV))),
        (row(conv_ln_w[0]), const_spec((1, D_CONV))),
        (row(conv_ln_b[0]), const_spec((1, D_CONV))),
        (wb_odd_pitch(w_out), const_spec((D_MODEL, padded(D_MODEL)))),
        (row(mix_post_norm[0]), const_spec((1, D_MODEL))),
        (row(ffn_pre_norm[0]), const_spec((1, D_MODEL))),
        (wb(w_ffn_gate), const_spec((D_MODEL, D_FF))),
        (wb(w_ffn_up), const_spec((D_MODEL, D_FF))),
        (wb_odd_pitch(w_ffn_down), const_spec((D_FF, padded(D_MODEL)))),
        (row(ffn_post_norm[0]), const_spec((1, D_MODEL))),
        (wb_odd_pitch(w_ple_gate), const_spec((D_MODEL, padded(D_MODEL)))),
        (wb_odd_pitch(w_ple_proj), const_spec((D_PLE, padded(D_MODEL)))),
        (row(ple_post_norm[0]), const_spec((1, D_MODEL))),
    ]
    args = [a for a, _ in operands]
    in_specs = [s for _, s in operands]

    out = pl.pallas_call(
        functools.partial(_block_kernel, tiles_per_row),
        grid=(n_tiles + 1,),
        in_specs=in_specs,
        out_specs=pl.BlockSpec((ts, D_MODEL), lambda n: (channel_tile(n), 0)),
        out_shape=jax.ShapeDtypeStruct((batch * seq, D_MODEL), x.dtype),
        scratch_shapes=[
            pltpu.VMEM((ts, D_IN), F32),
            pltpu.VMEM((SUBLANES, HALO + ts, D_CONV), F32),
            pltpu.VMEM((ts, D_MODEL), BF16),
            pltpu.VMEM((ts, D_FF), BF16),
            pltpu.VMEM((N_HEADS, HEAD_DIM, HEAD_DIM), F32),
        ],
        compiler_params=pltpu.CompilerParams(
            dimension_semantics=("arbitrary",),
            vmem_limit_bytes=VMEM_LIMIT_BYTES,
        ),
        name="hymba_block",
    )(*args)
    return out.reshape(batch, seq, D_MODEL)
```

```python
import functools

import jax
import jax.numpy as jnp
import numpy as np
from jax import lax
from jax.experimental import pallas as pl
from jax.experimental.pallas import tpu as pltpu

D_MODEL = 1024
D_CONV = 512
D_RET = 512
N_HEADS = 4
HEAD_DIM = 128
CONV_WIDTH = 31
CHUNK = 128
D_FF = 2816
D_PLE = 256
D_IN = 2 * D_CONV + 4 * D_RET
ROPE_BASE = 10000.0
EPS = 1e-6

SUBLANES = 8
LANES = 128
SEQ_TILE = 256
HALO = 32
CONV_ROWS = 64
FFN_COLS = 256
DOWN_COLS = 256
VMEM_LIMIT_BYTES = 56 * 1024 * 1024

SEGMENT_ORDER = (
    "CMMC "
    "CMCMCMCMCMCMCCCCC "
    "CCCC "
    "MC "
    "CMMMC"
)

Q0 = 2 * D_CONV
K0 = Q0 + D_RET
V0 = K0 + D_RET
G0 = V0 + D_RET

F32 = jnp.float32
BF16 = jnp.bfloat16
_EXHAUSTED = object()


def _rmsnorm(x, w):
    ms = jnp.mean(x * x, axis=-1, keepdims=True)
    return (x * lax.rsqrt(ms + EPS)) * w


def _silu(x):
    return x * jax.nn.sigmoid(x)


def _dot(a, b):
    return jnp.dot(a, b, preferred_element_type=F32)


def _mixer_stage(cd_ref, x_ref, cos_ref, sin_ref, dmask_ref, qd_ref, kd_ref,
                 mix_pre_ref, w_in_ref, dw_w_ref, dw_b_ref, ln_w_ref, ln_b_ref,
                 u_ref, glu_ref, state_ref, mix_ref):
    ts = x_ref.shape[0]
    h = _rmsnorm(x_ref[...], mix_pre_ref[...]).astype(BF16)
    yield
    u_ref[:, 0:Q0] = _dot(h, w_in_ref[:, 0:Q0])
    yield
    glu_ref[0, HALO:HALO + ts, :] = u_ref[:, 0:D_CONV] * jax.nn.sigmoid(u_ref[:, D_CONV:2 * D_CONV])
    yield
    for b in range(1, SUBLANES):
        glu_ref[b, SUBLANES:HALO + ts, :] = glu_ref[0, SUBLANES - b:HALO + ts - b, :]
    yield
    groups = CONV_ROWS // SUBLANES
    for r0 in range(0, ts, CONV_ROWS):
        accs = [dw_b_ref[...]] * groups
        for b in range(SUBLANES):
            n_a = (CONV_WIDTH - 1 - b) // SUBLANES + 1
            base = HALO + r0 - SUBLANES * (n_a - 1)
            w_taps = [dw_w_ref[CONV_WIDTH - 1 - (SUBLANES * a + b)] for a in range(n_a)]
            for j in range(groups + n_a - 1):
                row = glu_ref[b, base + SUBLANES * j:base + SUBLANES * (j + 1), :]
                for a in range(n_a):
                    i = j - (n_a - 1) + a
                    if 0 <= i < groups:
                        accs[i] = accs[i] + row * w_taps[a]
        acc = jnp.concatenate(accs, axis=0)
        mu = jnp.mean(acc, axis=-1, keepdims=True)
        cen = acc - mu
        var = jnp.mean(cen * cen, axis=-1, keepdims=True)
        yn = (cen * lax.rsqrt(var + EPS)) * ln_w_ref[...] + ln_b_ref[...]
        mix_ref[r0:r0 + CONV_ROWS, 0:D_CONV] = _silu(yn).astype(BF16)
        if r0 + CONV_ROWS == ts:
            glu_ref[0, 0:HALO, :] = glu_ref[0, ts:ts + HALO, :]
        yield

    u_ref[:, Q0:D_IN] = _dot(h, w_in_ref[:, Q0:D_IN])
    yield
    scale = HEAD_DIM ** -0.5
    units = [(c0, hd) for c0 in range(0, ts, CHUNK) for hd in range(N_HEADS)]
    qbs, vs, scores, carried = {}, {}, {}, {}
    for c0, hd in units:
        lo = hd * HEAD_DIM
        cosf = cos_ref[c0:c0 + CHUNK, :]
        sinf = sin_ref[c0:c0 + CHUNK, :]
        q = u_ref[c0:c0 + CHUNK, Q0 + lo:Q0 + lo + HEAD_DIM]
        k = u_ref[c0:c0 + CHUNK, K0 + lo:K0 + lo + HEAD_DIM]
        v = u_ref[c0:c0 + CHUNK, V0 + lo:V0 + lo + HEAD_DIM].astype(BF16)
        q = q * cosf + pltpu.roll(q, HEAD_DIM // 2, 1) * sinf
        k = (k * cosf + pltpu.roll(k, HEAD_DIM // 2, 1) * sinf) * scale
        qb = q.astype(BF16)
        scores[c0, hd] = lax.dot_general(qb, k.astype(BF16), (((1,), (1,)), ((), ())),
                                         preferred_element_type=F32)
        kdec = (k * kd_ref[:, lo:lo + HEAD_DIM]).astype(BF16)
        kv = lax.dot_general(kdec, v, (((0,), (0,)), ((), ())), preferred_element_type=F32)
        state = state_ref[hd]
        carried[c0, hd] = state.astype(BF16)
        state_ref[hd] = state * cd_ref[hd] + kv
        qbs[c0, hd], vs[c0, hd] = qb, v
    yield
    outs = {}
    for c0, hd in units:
        lo = hd * HEAD_DIM
        sc = (scores[c0, hd] * dmask_ref[hd]).astype(BF16)
        inner = _dot(sc, vs[c0, hd])
        cross = _dot(qbs[c0, hd], carried[c0, hd]) * qd_ref[:, lo:lo + HEAD_DIM]
        outs[c0, hd] = inner + cross
    yield
    for c0, hd in units:
        lo = hd * HEAD_DIM
        o = outs[c0, hd]
        o = o * lax.rsqrt(jnp.mean(o * o, axis=-1, keepdims=True) + EPS)
        g = u_ref[c0:c0 + CHUNK, G0 + lo:G0 + lo + HEAD_DIM]
        mix_ref[c0:c0 + CHUNK, D_CONV + lo:D_CONV + lo + HEAD_DIM] = (_silu(g) * o).astype(BF16)
    yield


def _channel_stage(mix_ref, xc_ref, p_ref, w_out_ref, mix_post_ref, ffn_pre_ref, w_gate_ref,
                   w_up_ref, w_down_ref, ffn_post_ref, w_pg_ref, w_pp_ref, ple_post_ref,
                   o_ref, act_ref):
    y = _dot(mix_ref[...], w_out_ref[:, 0:D_MODEL])
    yield
    x = xc_ref[...] + _rmsnorm(y, mix_post_ref[...])
    h = _rmsnorm(x, ffn_pre_ref[...]).astype(BF16)
    yield
    for c0 in range(0, D_FF, FFN_COLS):
        gate = _dot(h, w_gate_ref[:, c0:c0 + FFN_COLS])
        up = _dot(h, w_up_ref[:, c0:c0 + FFN_COLS])
        act_ref[:, c0:c0 + FFN_COLS] = (_silu(gate) * up).astype(BF16)
        yield
    parts = []
    for c0 in range(0, D_MODEL, DOWN_COLS):
        parts.append(_dot(act_ref[...], w_down_ref[:, c0:c0 + DOWN_COLS]))
        yield
    x = x + _rmsnorm(jnp.concatenate(parts, axis=1), ffn_post_ref[...])
    yield
    e_gate = _dot(x.astype(BF16), w_pg_ref[:, 0:D_MODEL])
    e_proj = _dot(p_ref[...].astype(BF16), w_pp_ref[:, 0:D_MODEL])
    yield
    e = jax.nn.sigmoid(e_gate) * e_proj
    o_ref[...] = x + _rmsnorm(e, ple_post_ref[...])
    yield


def _block_kernel(tiles_per_row,
                  cd_ref, x_ref, xc_ref, p_ref, cos_ref, sin_ref, dmask_ref, qd_ref, kd_ref,
                  mix_pre_ref, w_in_ref, dw_w_ref, dw_b_ref, ln_w_ref, ln_b_ref,
                  w_out_ref, mix_post_ref, ffn_pre_ref, w_gate_ref, w_up_ref,
                  w_down_ref, ffn_post_ref, w_pg_ref, w_pp_ref, ple_post_ref,
                  o_ref,
                  u_ref, glu_ref, mix_ref, act_ref, state_ref):
    n = pl.program_id(0)

    @pl.when(n == 0)
    def _():
        mix_ref[...] = jnp.zeros_like(mix_ref)

    @pl.when(n % tiles_per_row == 0)
    def _():
        state_ref[...] = jnp.zeros_like(state_ref)
        glu_ref[0, 0:HALO, :] = jnp.zeros((HALO, D_CONV), F32)

    stages = {
        "C": _channel_stage(mix_ref, xc_ref, p_ref, w_out_ref, mix_post_ref, ffn_pre_ref,
                            w_gate_ref, w_up_ref, w_down_ref, ffn_post_ref, w_pg_ref, w_pp_ref,
                            ple_post_ref, o_ref, act_ref),
        "M": _mixer_stage(cd_ref, x_ref, cos_ref, sin_ref, dmask_ref, qd_ref, kd_ref,
                          mix_pre_ref, w_in_ref, dw_w_ref, dw_b_ref, ln_w_ref, ln_b_ref,
                          u_ref, glu_ref, state_ref, mix_ref),
    }
    for key in SEGMENT_ORDER.replace(" ", ""):
        next(stages[key])
    for stage in stages.values():
        assert next(stage, _EXHAUSTED) is _EXHAUSTED, "SEGMENT_ORDER leaves a stage unfinished"


def _retention_tables(seq):
    f32 = np.float32
    half = HEAD_DIM // 2
    pos = np.arange(seq, dtype=f32)
    freqs = (f32(ROPE_BASE) ** (-np.arange(half, dtype=f32) / f32(half))).astype(f32)
    ang = (pos[:, None] * freqs[None, :]).astype(f32)
    cos, sin = np.cos(ang).astype(f32), np.sin(ang).astype(f32)
    cosf = np.concatenate([cos, cos], axis=-1)
    sinf = np.concatenate([-sin, sin], axis=-1)
    log_g = np.log(f32(1.0) - f32(2.0) ** (f32(-5.0) - np.arange(N_HEADS, dtype=f32))).astype(f32)
    idx = np.arange(CHUNK, dtype=f32)
    rel = idx[:, None] - idx[None, :]
    dmask = np.where(rel >= 0, np.exp(log_g[:, None, None] * np.maximum(rel, f32(0.0))), f32(0.0)).astype(f32)
    q_decay = np.exp(log_g[:, None] * (idx + f32(1.0))).astype(f32)
    k_decay = np.exp(log_g[:, None] * (f32(CHUNK - 1.0) - idx)).astype(f32)
    chunk_decay = np.exp(log_g * f32(CHUNK)).astype(f32)
    widen = lambda t: np.repeat(t.T, HEAD_DIM, axis=1)
    tables = (cosf, sinf, dmask, widen(q_decay), widen(k_decay), chunk_decay)
    return tuple(jnp.asarray(t) for t in tables)


def kernel(x, p, mix_pre_norm, w_in, conv_dw_w, conv_dw_b, conv_ln_w, conv_ln_b, w_out,
           mix_post_norm, ffn_pre_norm, w_ffn_gate, w_ffn_up, w_ffn_down, ffn_post_norm,
           w_ple_gate, w_ple_proj, ple_post_norm):
    batch, seq, d_model = x.shape
    depth = p.shape[0]
    assert depth == 1 and d_model == D_MODEL and seq % SEQ_TILE == 0 and SEQ_TILE % CHUNK == 0
    ts = SEQ_TILE
    tiles_per_row = seq // ts
    n_tiles = batch * tiles_per_row
    cosf, sinf, dmask, qd, kd, cd = _retention_tables(seq)

    row = lambda v: v.reshape(1, -1)
    wb = lambda w: w[0].astype(BF16)

    def wb_odd_pitch(w):
        spare = LANES if (w.shape[2] // LANES) % SUBLANES == 0 else 0
        return jnp.pad(w[0], ((0, 0), (0, spare))).astype(BF16)

    padded = lambda cols: cols + (LANES if (cols // LANES) % SUBLANES == 0 else 0)

    mixer_tile = lambda n: jnp.minimum(n, n_tiles - 1)
    channel_tile = lambda n: jnp.maximum(n - 1, 0)

    def const_spec(shape):
        zeros = (0,) * len(shape)
        return pl.BlockSpec(shape, lambda n: zeros, pipeline_mode=pl.Buffered(1))

    rope_spec = pl.BlockSpec((ts, HEAD_DIM), lambda n: (mixer_tile(n) % tiles_per_row, 0))
    x_flat = x.reshape(batch * seq, D_MODEL)
    operands = [
        (cd, pl.BlockSpec(memory_space=pltpu.SMEM)),
        (x_flat, pl.BlockSpec((ts, D_MODEL), lambda n: (mixer_tile(n), 0))),
        (x_flat, pl.BlockSpec((ts, D_MODEL), lambda n: (channel_tile(n), 0))),
        (p.reshape(batch * seq, D_PLE), pl.BlockSpec((ts, D_PLE), lambda n: (channel_tile(n), 0))),
        (cosf, rope_spec),
        (sinf, rope_spec),
        (dmask, const_spec((N_HEADS, CHUNK, CHUNK))),
        (qd, const_spec((CHUNK, D_RET))),
        (kd, const_spec((CHUNK, D_RET))),
        (row(mix_pre_norm[0]), const_spec((1, D_MODEL))),
        (wb_odd_pitch(w_in), const_spec((D_MODEL, padded(D_IN)))),
        (jnp.broadcast_to(conv_dw_w[0][:, None, :], (CONV_WIDTH, SUBLANES, D_CONV)),
         const_spec((CONV_WIDTH, SUBLANES, D_CONV))),
        (jnp.broadcast_to(conv_dw_b[0][None, :], (SUBLANES, D_CONV)), const_spec((SUBLANES, D_CONV))),
        (row(conv_ln_w[0]), const_spec((1, D_CONV))),
        (row(conv_ln_b[0]), const_spec((1, D_CONV))),
        (wb_odd_pitch(w_out), const_spec((D_MODEL, padded(D_MODEL)))),
        (row(mix_post_norm[0]), const_spec((1, D_MODEL))),
        (row(ffn_pre_norm[0]), const_spec((1, D_MODEL))),
        (wb(w_ffn_gate), const_spec((D_MODEL, D_FF))),
        (wb(w_ffn_up), const_spec((D_MODEL, D_FF))),
        (wb_odd_pitch(w_ffn_down), const_spec((D_FF, padded(D_MODEL)))),
        (row(ffn_post_norm[0]), const_spec((1, D_MODEL))),
        (wb_odd_pitch(w_ple_gate), const_spec((D_MODEL, padded(D_MODEL)))),
        (wb_odd_pitch(w_ple_proj), const_spec((D_PLE, padded(D_MODEL)))),
        (row(ple_post_norm[0]), const_spec((1, D_MODEL))),
    ]
    args = [a for a, _ in operands]
    in_specs = [s for _, s in operands]

    out = pl.pallas_call(
        functools.partial(_block_kernel, tiles_per_row),
        grid=(n_tiles + 1,),
        in_specs=in_specs,
        out_specs=pl.BlockSpec((ts, D_MODEL), lambda n: (channel_tile(n), 0)),
        out_shape=jax.ShapeDtypeStruct((batch * seq, D_MODEL), x.dtype),
        scratch_shapes=[
            pltpu.VMEM((ts, D_IN), F32),
            pltpu.VMEM((SUBLANES, HALO + ts, D_CONV), F32),
            pltpu.VMEM((ts, D_MODEL), BF16),
            pltpu.VMEM((ts, D_FF), BF16),
            pltpu.VMEM((N_HEADS, HEAD_DIM, HEAD_DIM), F32),
        ],
        compiler_params=pltpu.CompilerParams(
            dimension_semantics=("arbitrary",),
            vmem_limit_bytes=VMEM_LIMIT_BYTES,
        ),
        name="hymba_block",
    )(*args)
    return out.reshape(batch, seq, D_MODEL)
```

```python
import functools

import jax
import jax.numpy as jnp
import numpy as np
from jax import lax
from jax.experimental import pallas as pl
from jax.experimental.pallas import tpu as pltpu

D_MODEL = 1024
D_CONV = 512
D_RET = 512
N_HEADS = 4
HEAD_DIM = 128
CONV_WIDTH = 31
CHUNK = 128
D_FF = 2816
D_PLE = 256
D_IN = 2 * D_CONV + 4 * D_RET
ROPE_BASE = 10000.0
EPS = 1e-6

SUBLANES = 8
LANES = 128
SEQ_TILE = 256
HALO = 32
CONV_ROWS = 64
FFN_COLS = 256
DOWN_COLS = 256
VMEM_LIMIT_BYTES = 56 * 1024 * 1024

SEGMENT_ORDER = (
    "CMMC "
    "CMCMCMCMCMCMCCCCC "
    "CCCC "
    "MC "
    "CMMMC"
)

Q0 = 2 * D_CONV
K0 = Q0 + D_RET
V0 = K0 + D_RET
G0 = V0 + D_RET

F32 = jnp.float32
BF16 = jnp.bfloat16
_EXHAUSTED = object()


def _rmsnorm(x, w):
    ms = jnp.mean(x * x, axis=-1, keepdims=True)
    return (x * lax.rsqrt(ms + EPS)) * w


def _silu(x):
    return x * jax.nn.sigmoid(x)


def _dot(a, b):
    return jnp.dot(a, b, preferred_element_type=F32)


def _mixer_stage(cd_ref, x_ref, cos_ref, sin_ref, dmask_ref, qd_ref, kd_ref,
                 mix_pre_ref, w_in_ref, dw_w_ref, dw_b_ref, ln_w_ref, ln_b_ref,
                 u_ref, glu_ref, state_ref, mix_ref):
    ts = x_ref.shape[0]
    h = _rmsnorm(x_ref[...], mix_pre_ref[...]).astype(BF16)
    yield
    u_ref[:, 0:Q0] = _dot(h, w_in_ref[:, 0:Q0])
    yield
    glu_ref[0, HALO:HALO + ts, :] = u_ref[:, 0:D_CONV] * jax.nn.sigmoid(u_ref[:, D_CONV:2 * D_CONV])
    yield
    for b in range(1, SUBLANES):
        glu_ref[b, SUBLANES:HALO + ts, :] = glu_ref[0, SUBLANES - b:HALO + ts - b, :]
    yield
    groups = CONV_ROWS // SUBLANES
    for r0 in range(0, ts, CONV_ROWS):
        accs = [dw_b_ref[...]] * groups
        for b in range(SUBLANES):
            n_a = (CONV_WIDTH - 1 - b) // SUBLANES + 1
            base = HALO + r0 - SUBLANES * (n_a - 1)
            w_taps = [dw_w_ref[CONV_WIDTH - 1 - (SUBLANES * a + b)] for a in range(n_a)]
            for j in range(groups + n_a - 1):
                row = glu_ref[b, base + SUBLANES * j:base + SUBLANES * (j + 1), :]
                for a in range(n_a):
                    i = j - (n_a - 1) + a
                    if 0 <= i < groups:
                        accs[i] = accs[i] + row * w_taps[a]
        acc = jnp.concatenate(accs, axis=0)
        mu = jnp.mean(acc, axis=-1, keepdims=True)
        cen = acc - mu
        var = jnp.mean(cen * cen, axis=-1, keepdims=True)
        yn = (cen * lax.rsqrt(var + EPS)) * ln_w_ref[...] + ln_b_ref[...]
        mix_ref[r0:r0 + CONV_ROWS, 0:D_CONV] = _silu(yn).astype(BF16)
        if r0 + CONV_ROWS == ts:
            glu_ref[0, 0:HALO, :] = glu_ref[0, ts:ts + HALO, :]
        yield

    u_ref[:, Q0:D_IN] = _dot(h, w_in_ref[:, Q0:D_IN])
    yield
    scale = HEAD_DIM ** -0.5
    units = [(c0, hd) for c0 in range(0, ts, CHUNK) for hd in range(N_HEADS)]
    qbs, vs, scores, carried = {}, {}, {}, {}
    for c0, hd in units:
        lo = hd * HEAD_DIM
        cosf = cos_ref[c0:c0 + CHUNK, :]
        sinf = sin_ref[c0:c0 + CHUNK, :]
        q = u_ref[c0:c0 + CHUNK, Q0 + lo:Q0 + lo + HEAD_DIM]
        k = u_ref[c0:c0 + CHUNK, K0 + lo:K0 + lo + HEAD_DIM]
        v = u_ref[c0:c0 + CHUNK, V0 + lo:V0 + lo + HEAD_DIM].astype(BF16)
        q = q * cosf + pltpu.roll(q, HEAD_DIM // 2, 1) * sinf
        k = (k * cosf + pltpu.roll(k, HEAD_DIM // 2, 1) * sinf) * scale
        qb = q.astype(BF16)
        scores[c0, hd] = lax.dot_general(qb, k.astype(BF16), (((1,), (1,)), ((), ())),
                                         preferred_element_type=F32)
        kdec = (k * kd_ref[:, lo:lo + HEAD_DIM]).astype(BF16)
        kv = lax.dot_general(kdec, v, (((0,), (0,)), ((), ())), preferred_element_type=F32)
        state = state_ref[hd]
        carried[c0, hd] = state.astype(BF16)
        state_ref[hd] = state * cd_ref[hd] + kv
        qbs[c0, hd], vs[c0, hd] = qb, v
    yield
    outs = {}
    for c0, hd in units:
        lo = hd * HEAD_DIM
        sc = (scores[c0, hd] * dmask_ref[hd]).astype(BF16)
        inner = _dot(sc, vs[c0, hd])
        cross = _dot(qbs[c0, hd], carried[c0, hd]) * qd_ref[:, lo:lo + HEAD_DIM]
        outs[c0, hd] = inner + cross
    yield
    for c0, hd in units:
        lo = hd * HEAD_DIM
        o = outs[c0, hd]
        o = o * lax.rsqrt(jnp.mean(o * o, axis=-1, keepdims=True) + EPS)
        g = u_ref[c0:c0 + CHUNK, G0 + lo:G0 + lo + HEAD_DIM]
        mix_ref[c0:c0 + CHUNK, D_CONV + lo:D_CONV + lo + HEAD_DIM] = (_silu(g) * o).astype(BF16)
    yield


def _channel_stage(mix_ref, xc_ref, p_ref, w_out_ref, mix_post_ref, ffn_pre_ref, w_gate_ref,
                   w_up_ref, w_down_ref, ffn_post_ref, w_pg_ref, w_pp_ref, ple_post_ref,
                   o_ref, act_ref):
    y = _dot(mix_ref[...], w_out_ref[:, 0:D_MODEL])
    yield
    x = xc_ref[...] + _rmsnorm(y, mix_post_ref[...])
    h = _rmsnorm(x, ffn_pre_ref[...]).astype(BF16)
    yield
    for c0 in range(0, D_FF, FFN_COLS):
        gate = _dot(h, w_gate_ref[:, c0:c0 + FFN_COLS])
        up = _dot(h, w_up_ref[:, c0:c0 + FFN_COLS])
        act_ref[:, c0:c0 + FFN_COLS] = (_silu(gate) * up).astype(BF16)
        yield
    parts = []
    for c0 in range(0, D_MODEL, DOWN_COLS):
        parts.append(_dot(act_ref[...], w_down_ref[:, c0:c0 + DOWN_COLS]))
        yield
    x = x + _rmsnorm(jnp.concatenate(parts, axis=1), ffn_post_ref[...])
    yield
    e_gate = _dot(x.astype(BF16), w_pg_ref[:, 0:D_MODEL])
    e_proj = _dot(p_ref[...].astype(BF16), w_pp_ref[:, 0:D_MODEL])
    yield
    e = jax.nn.sigmoid(e_gate) * e_proj
    o_ref[...] = x + _rmsnorm(e, ple_post_ref[...])
    yield


def _block_kernel(tiles_per_row,
                  cd_ref, x_ref, xc_ref, p_ref, cos_ref, sin_ref, dmask_ref, qd_ref, kd_ref,
                  mix_pre_ref, w_in_ref, dw_w_ref, dw_b_ref, ln_w_ref, ln_b_ref,
                  w_out_ref, mix_post_ref, ffn_pre_ref, w_gate_ref, w_up_ref,
                  w_down_ref, ffn_post_ref, w_pg_ref, w_pp_ref, ple_post_ref,
                  o_ref,
                  u_ref, glu_ref, mix_ref, act_ref, state_ref):
    n = pl.program_id(0)

    @pl.when(n == 0)
    def _():
        mix_ref[...] = jnp.zeros_like(mix_ref)

    @pl.when(n % tiles_per_row == 0)
    def _():
        state_ref[...] = jnp.zeros_like(state_ref)
        glu_ref[0, 0:HALO, :] = jnp.zeros((HALO, D_CONV), F32)

    stages = {
        "C": _channel_stage(mix_ref, xc_ref, p_ref, w_out_ref, mix_post_ref, ffn_pre_ref,
                            w_gate_ref, w_up_ref, w_down_ref, ffn_post_ref, w_pg_ref, w_pp_ref,
                            ple_post_ref, o_ref, act_ref),
        "M": _mixer_stage(cd_ref, x_ref, cos_ref, sin_ref, dmask_ref, qd_ref, kd_ref,
                          mix_pre_ref, w_in_ref, dw_w_ref, dw_b_ref, ln_w_ref, ln_b_ref,
                          u_ref, glu_ref, state_ref, mix_ref),
    }
    for key in SEGMENT_ORDER.replace(" ", ""):
        next(stages[key])
    for stage in stages.values():
        assert next(stage, _EXHAUSTED) is _EXHAUSTED, "SEGMENT_ORDER leaves a stage unfinished"


def _retention_tables(seq):
    f32 = np.float32
    half = HEAD_DIM // 2
    pos = np.arange(seq, dtype=f32)
    freqs = (f32(ROPE_BASE) ** (-np.arange(half, dtype=f32) / f32(half))).astype(f32)
    ang = (pos[:, None] * freqs[None, :]).astype(f32)
    cos, sin = np.cos(ang).astype(f32), np.sin(ang).astype(f32)
    cosf = np.concatenate([cos, cos], axis=-1)
    sinf = np.concatenate([-sin, sin], axis=-1)
    log_g = np.log(f32(1.0) - f32(2.0) ** (f32(-5.0) - np.arange(N_HEADS, dtype=f32))).astype(f32)
    idx = np.arange(CHUNK, dtype=f32)
    rel = idx[:, None] - idx[None, :]
    dmask = np.where(rel >= 0, np.exp(log_g[:, None, None] * np.maximum(rel, f32(0.0))), f32(0.0)).astype(f32)
    q_decay = np.exp(log_g[:, None] * (idx + f32(1.0))).astype(f32)
    k_decay = np.exp(log_g[:, None] * (f32(CHUNK - 1.0) - idx)).astype(f32)
    chunk_decay = np.exp(log_g * f32(CHUNK)).astype(f32)
    widen = lambda t: np.repeat(t.T, HEAD_DIM, axis=1)
    tables = (cosf, sinf, dmask, widen(q_decay), widen(k_decay), chunk_decay)
    return tuple(jnp.asarray(t) for t in tables)


def kernel(x, p, mix_pre_norm, w_in, conv_dw_w, conv_dw_b, conv_ln_w, conv_ln_b, w_out,
           mix_post_norm, ffn_pre_norm, w_ffn_gate, w_ffn_up, w_ffn_down, ffn_post_norm,
           w_ple_gate, w_ple_proj, ple_post_norm):
    batch, seq, d_model = x.shape
    depth = p.shape[0]
    assert depth == 1 and d_model == D_MODEL and seq % SEQ_TILE == 0 and SEQ_TILE % CHUNK == 0
    ts = SEQ_TILE
    tiles_per_row = seq // ts
    n_tiles = batch * tiles_per_row
    cosf, sinf, dmask, qd, kd, cd = _retention_tables(seq)

    row = lambda v: v.reshape(1, -1)
    wb = lambda w: w[0].astype(BF16)

    def wb_odd_pitch(w):
        spare = LANES if (w.shape[2] // LANES) % SUBLANES == 0 else 0
        return jnp.pad(w[0], ((0, 0), (0, spare))).astype(BF16)

    padded = lambda cols: cols + (LANES if (cols // LANES) % SUBLANES == 0 else 0)

    mixer_tile = lambda n: jnp.minimum(n, n_tiles - 1)
    channel_tile = lambda n: jnp.maximum(n - 1, 0)

    def const_spec(shape):
        zeros = (0,) * len(shape)
        return pl.BlockSpec(shape, lambda n: zeros, pipeline_mode=pl.Buffered(1))

    rope_spec = pl.BlockSpec((ts, HEAD_DIM), lambda n: (mixer_tile(n) % tiles_per_row, 0))
    x_flat = x.reshape(batch * seq, D_MODEL)
    operands = [
        (cd, pl.BlockSpec(memory_space=pltpu.SMEM)),
        (x_flat, pl.BlockSpec((ts, D_MODEL), lambda n: (mixer_tile(n), 0))),
        (x_flat, pl.BlockSpec((ts, D_MODEL), lambda n: (channel_tile(n), 0))),
        (p.reshape(batch * seq, D_PLE), pl.BlockSpec((ts, D_PLE), lambda n: (channel_tile(n), 0))),
        (cosf, rope_spec),
        (sinf, rope_spec),
        (dmask, const_spec((N_HEADS, CHUNK, CHUNK))),
        (qd, const_spec((CHUNK, D_RET))),
        (kd, const_spec((CHUNK, D_RET))),
        (row(mix_pre_norm[0]), const_spec((1, D_MODEL))),
        (wb_odd_pitch(w_in), const_spec((D_MODEL, padded(D_IN)))),
        (jnp.broadcast_to(conv_dw_w[0][:, None, :], (CONV_WIDTH, SUBLANES, D_CONV)),
         const_spec((CONV_WIDTH, SUBLANES, D_CONV))),
        (jnp.broadcast_to(conv_dw_b[0][None, :], (SUBLANES, D_CONV)), const_spec((SUBLANES, D_CONV))),
        (row(conv_ln_w[0]), const_spec((1, D_CONV))),
        (row(conv_ln_b[0]), const_spec((1, D_CONV))),
        (wb_odd_pitch(w_out), const_spec((D_MODEL, padded(D_MODEL)))),
        (row(mix_post_norm[0]), const_spec((1, D_MODEL))),
        (row(ffn_pre_norm[0]), const_spec((1, D_MODEL))),
        (wb(w_ffn_gate), const_spec((D_MODEL, D_FF))),
        (wb(w_ffn_up), const_spec((D_MODEL, D_FF))),
        (wb_odd_pitch(w_ffn_down), const_spec((D_FF, padded(D_MODEL)))),
        (row(ffn_post_norm[0]), const_spec((1, D_MODEL))),
        (wb_odd_pitch(w_ple_gate), const_spec((D_MODEL, padded(D_MODEL)))),
        (wb_odd_pitch(w_ple_proj), const_spec((D_PLE, padded(D_MODEL)))),
        (row(ple_post_norm[0]), const_spec((1, D_MODEL))),
    ]
    args = [a for a, _ in operands]
    in_specs = [s for _, s in operands]

    out = pl.pallas_call(
        functools.partial(_block_kernel, tiles_per_row),
        grid=(n_tiles + 1,),
        in_specs=in_specs,
        out_specs=pl.BlockSpec((ts, D_MODEL), lambda n: (channel_tile(n), 0)),
        out_shape=jax.ShapeDtypeStruct((batch * seq, D_MODEL), x.dtype),
        scratch_shapes=[
            pltpu.VMEM((ts, D_IN), F32),
            pltpu.VMEM((SUBLANES, HALO + ts, D_CONV), F32),
            pltpu.VMEM((ts, D_MODEL), BF16),
            pltpu.VMEM((ts, D_FF), BF16),
            pltpu.VMEM((N_HEADS, HEAD_DIM, HEAD_DIM), F32),
        ],
        compiler_params=pltpu.CompilerParams(
            dimension_semantics=("arbitrary",),
            vmem_limit_bytes=VMEM_LIMIT_BYTES,
            allow_input_fusion=[a.dtype == BF16 for a in args],
        ),
        name="hymba_block",
    )(*args)
    return out.reshape(batch, seq, D_MODEL)
```

```python
import functools

import jax
import jax.numpy as jnp
import numpy as np
from jax import lax
from jax.experimental import pallas as pl
from jax.experimental.pallas import tpu as pltpu

D_MODEL = 1024
D_CONV = 512
D_RET = 512
N_HEADS = 4
HEAD_DIM = 128
CONV_WIDTH = 31
CHUNK = 128
D_FF = 2816
D_PLE = 256
D_IN = 2 * D_CONV + 4 * D_RET
ROPE_BASE = 10000.0
EPS = 1e-6

SUBLANES = 8
LANES = 128
SEQ_TILE = 256
HALO = 32
CONV_ROWS = 64
FFN_COLS = 256
DOWN_COLS = 256
WEIGHT_ROWS = 256
VMEM_LIMIT_BYTES = 56 * 1024 * 1024

SEGMENT_ORDER = (
    "CMMC "
    "CMCMCMCMCMCMCCCCC "
    "CCCC "
    "MC "
    "CMMMC"
)

Q0 = 2 * D_CONV
K0 = Q0 + D_RET
V0 = K0 + D_RET
G0 = V0 + D_RET

F32 = jnp.float32
BF16 = jnp.bfloat16
_EXHAUSTED = object()


def _rmsnorm(x, w):
    ms = jnp.mean(x * x, axis=-1, keepdims=True)
    return (x * lax.rsqrt(ms + EPS)) * w


def _silu(x):
    return x * jax.nn.sigmoid(x)


def _dot(a, b):
    return jnp.dot(a, b, preferred_element_type=F32)


def _load_weight_bf16(w_hbm, w_ref, stage_ref, stage_sem):
    rows, cols = w_hbm.shape
    n_slabs = rows // WEIGHT_ROWS

    def slab_copy(i, slot):
        return pltpu.make_async_copy(w_hbm.at[pl.ds(i * WEIGHT_ROWS, WEIGHT_ROWS), :],
                                     stage_ref.at[slot, :, 0:cols], stage_sem.at[slot])

    slab_copy(0, 0).start()

    def body(i, carry):
        slot = i % 2

        @pl.when(i + 1 < n_slabs)
        def _():
            slab_copy(i + 1, 1 - slot).start()

        slab_copy(i, slot).wait()
        w_ref[pl.ds(pl.multiple_of(i * WEIGHT_ROWS, WEIGHT_ROWS), WEIGHT_ROWS), 0:cols] = (
            stage_ref[slot, :, 0:cols].astype(BF16))
        return carry

    lax.fori_loop(0, n_slabs, body, 0)


def _mixer_stage(cd_ref, x_ref, cos_ref, sin_ref, dmask_ref, qd_ref, kd_ref,
                 mix_pre_ref, w_in_ref, dw_w_ref, dw_b_ref, ln_w_ref, ln_b_ref,
                 u_ref, glu_ref, state_ref, mix_ref):
    ts = x_ref.shape[0]
    h = _rmsnorm(x_ref[...], mix_pre_ref[...]).astype(BF16)
    yield
    u_ref[:, 0:Q0] = _dot(h, w_in_ref[:, 0:Q0])
    yield
    glu_ref[0, HALO:HALO + ts, :] = u_ref[:, 0:D_CONV] * jax.nn.sigmoid(u_ref[:, D_CONV:2 * D_CONV])
    yield
    for b in range(1, SUBLANES):
        glu_ref[b, SUBLANES:HALO + ts, :] = glu_ref[0, SUBLANES - b:HALO + ts - b, :]
    yield
    groups = CONV_ROWS // SUBLANES
    for r0 in range(0, ts, CONV_ROWS):
        accs = [dw_b_ref[...]] * groups
        for b in range(SUBLANES):
            n_a = (CONV_WIDTH - 1 - b) // SUBLANES + 1
            base = HALO + r0 - SUBLANES * (n_a - 1)
            w_taps = [dw_w_ref[CONV_WIDTH - 1 - (SUBLANES * a + b)] for a in range(n_a)]
            for j in range(groups + n_a - 1):
                row = glu_ref[b, base + SUBLANES * j:base + SUBLANES * (j + 1), :]
                for a in range(n_a):
                    i = j - (n_a - 1) + a
                    if 0 <= i < groups:
                        accs[i] = accs[i] + row * w_taps[a]
        acc = jnp.concatenate(accs, axis=0)
        mu = jnp.mean(acc, axis=-1, keepdims=True)
        cen = acc - mu
        var = jnp.mean(cen * cen, axis=-1, keepdims=True)
        yn = (cen * lax.rsqrt(var + EPS)) * ln_w_ref[...] + ln_b_ref[...]
        mix_ref[r0:r0 + CONV_ROWS, 0:D_CONV] = _silu(yn).astype(BF16)
        if r0 + CONV_ROWS == ts:
            glu_ref[0, 0:HALO, :] = glu_ref[0, ts:ts + HALO, :]
        yield

    u_ref[:, Q0:D_IN] = _dot(h, w_in_ref[:, Q0:D_IN])
    yield
    scale = HEAD_DIM ** -0.5
    units = [(c0, hd) for c0 in range(0, ts, CHUNK) for hd in range(N_HEADS)]
    qbs, vs, scores, carried = {}, {}, {}, {}
    for c0, hd in units:
        lo = hd * HEAD_DIM
        cosf = cos_ref[c0:c0 + CHUNK, :]
        sinf = sin_ref[c0:c0 + CHUNK, :]
        q = u_ref[c0:c0 + CHUNK, Q0 + lo:Q0 + lo + HEAD_DIM]
        k = u_ref[c0:c0 + CHUNK, K0 + lo:K0 + lo + HEAD_DIM]
        v = u_ref[c0:c0 + CHUNK, V0 + lo:V0 + lo + HEAD_DIM].astype(BF16)
        q = q * cosf + pltpu.roll(q, HEAD_DIM // 2, 1) * sinf
        k = (k * cosf + pltpu.roll(k, HEAD_DIM // 2, 1) * sinf) * scale
        qb = q.astype(BF16)
        scores[c0, hd] = lax.dot_general(qb, k.astype(BF16), (((1,), (1,)), ((), ())),
                                         preferred_element_type=F32)
        kdec = (k * kd_ref[:, lo:lo + HEAD_DIM]).astype(BF16)
        kv = lax.dot_general(kdec, v, (((0,), (0,)), ((), ())), preferred_element_type=F32)
        state = state_ref[hd]
        carried[c0, hd] = state.astype(BF16)
        state_ref[hd] = state * cd_ref[hd] + kv
        qbs[c0, hd], vs[c0, hd] = qb, v
    yield
    outs = {}
    for c0, hd in units:
        lo = hd * HEAD_DIM
        sc = (scores[c0, hd] * dmask_ref[hd]).astype(BF16)
        inner = _dot(sc, vs[c0, hd])
        cross = _dot(qbs[c0, hd], carried[c0, hd]) * qd_ref[:, lo:lo + HEAD_DIM]
        outs[c0, hd] = inner + cross
    yield
    for c0, hd in units:
        lo = hd * HEAD_DIM
        o = outs[c0, hd]
        o = o * lax.rsqrt(jnp.mean(o * o, axis=-1, keepdims=True) + EPS)
        g = u_ref[c0:c0 + CHUNK, G0 + lo:G0 + lo + HEAD_DIM]
        mix_ref[c0:c0 + CHUNK, D_CONV + lo:D_CONV + lo + HEAD_DIM] = (_silu(g) * o).astype(BF16)
    yield


def _channel_stage(mix_ref, xc_ref, p_ref, w_out_ref, mix_post_ref, ffn_pre_ref, w_gate_ref,
                   w_up_ref, w_down_ref, ffn_post_ref, w_pg_ref, w_pp_ref, ple_post_ref,
                   o_ref, act_ref):
    y = _dot(mix_ref[...], w_out_ref[:, 0:D_MODEL])
    yield
    x = xc_ref[...] + _rmsnorm(y, mix_post_ref[...])
    h = _rmsnorm(x, ffn_pre_ref[...]).astype(BF16)
    yield
    for c0 in range(0, D_FF, FFN_COLS):
        gate = _dot(h, w_gate_ref[:, c0:c0 + FFN_COLS])
        up = _dot(h, w_up_ref[:, c0:c0 + FFN_COLS])
        act_ref[:, c0:c0 + FFN_COLS] = (_silu(gate) * up).astype(BF16)
        yield
    parts = []
    for c0 in range(0, D_MODEL, DOWN_COLS):
        parts.append(_dot(act_ref[...], w_down_ref[:, c0:c0 + DOWN_COLS]))
        yield
    x = x + _rmsnorm(jnp.concatenate(parts, axis=1), ffn_post_ref[...])
    yield
    e_gate = _dot(x.astype(BF16), w_pg_ref[:, 0:D_MODEL])
    e_proj = _dot(p_ref[...].astype(BF16), w_pp_ref[:, 0:D_MODEL])
    yield
    e = jax.nn.sigmoid(e_gate) * e_proj
    o_ref[...] = x + _rmsnorm(e, ple_post_ref[...])
    yield


def _block_kernel(tiles_per_row,
                  cd_ref, x_ref, xc_ref, p_ref, cos_ref, sin_ref, dmask_ref, qd_ref, kd_ref,
                  mix_pre_ref, w_in_hbm, dw_w_ref, dw_b_ref, ln_w_ref, ln_b_ref,
                  w_out_hbm, mix_post_ref, ffn_pre_ref, w_gate_hbm, w_up_hbm,
                  w_down_hbm, ffn_post_ref, w_pg_hbm, w_pp_hbm, ple_post_ref,
                  o_ref,
                  u_ref, glu_ref, mix_ref, act_ref, state_ref,
                  w_in_ref, w_out_ref, w_gate_ref, w_up_ref, w_down_ref, w_pg_ref, w_pp_ref,
                  stage_ref, stage_sem):
    n = pl.program_id(0)

    @pl.when(n == 0)
    def _():
        mix_ref[...] = jnp.zeros_like(mix_ref)
        for hbm, vmem in ((w_in_hbm, w_in_ref), (w_out_hbm, w_out_ref), (w_gate_hbm, w_gate_ref),
                          (w_up_hbm, w_up_ref), (w_down_hbm, w_down_ref), (w_pg_hbm, w_pg_ref),
                          (w_pp_hbm, w_pp_ref)):
            _load_weight_bf16(hbm, vmem, stage_ref, stage_sem)

    @pl.when(n % tiles_per_row == 0)
    def _():
        state_ref[...] = jnp.zeros_like(state_ref)
        glu_ref[0, 0:HALO, :] = jnp.zeros((HALO, D_CONV), F32)

    stages = {
        "C": _channel_stage(mix_ref, xc_ref, p_ref, w_out_ref, mix_post_ref, ffn_pre_ref,
                            w_gate_ref, w_up_ref, w_down_ref, ffn_post_ref, w_pg_ref, w_pp_ref,
                            ple_post_ref, o_ref, act_ref),
        "M": _mixer_stage(cd_ref, x_ref, cos_ref, sin_ref, dmask_ref, qd_ref, kd_ref,
                          mix_pre_ref, w_in_ref, dw_w_ref, dw_b_ref, ln_w_ref, ln_b_ref,
                          u_ref, glu_ref, state_ref, mix_ref),
    }
    for key in SEGMENT_ORDER.replace(" ", ""):
        next(stages[key])
    for stage in stages.values():
        assert next(stage, _EXHAUSTED) is _EXHAUSTED, "SEGMENT_ORDER leaves a stage unfinished"


def _retention_tables(seq):
    f32 = np.float32
    half = HEAD_DIM // 2
    pos = np.arange(seq, dtype=f32)
    freqs = (f32(ROPE_BASE) ** (-np.arange(half, dtype=f32) / f32(half))).astype(f32)
    ang = (pos[:, None] * freqs[None, :]).astype(f32)
    cos, sin = np.cos(ang).astype(f32), np.sin(ang).astype(f32)
    cosf = np.concatenate([cos, cos], axis=-1)
    sinf = np.concatenate([-sin, sin], axis=-1)
    log_g = np.log(f32(1.0) - f32(2.0) ** (f32(-5.0) - np.arange(N_HEADS, dtype=f32))).astype(f32)
    idx = np.arange(CHUNK, dtype=f32)
    rel = idx[:, None] - idx[None, :]
    dmask = np.where(rel >= 0, np.exp(log_g[:, None, None] * np.maximum(rel, f32(0.0))), f32(0.0)).astype(f32)
    q_decay = np.exp(log_g[:, None] * (idx + f32(1.0))).astype(f32)
    k_decay = np.exp(log_g[:, None] * (f32(CHUNK - 1.0) - idx)).astype(f32)
    chunk_decay = np.exp(log_g * f32(CHUNK)).astype(f32)
    widen = lambda t: np.repeat(t.T, HEAD_DIM, axis=1)
    tables = (cosf, sinf, dmask, widen(q_decay), widen(k_decay), chunk_decay)
    return tuple(jnp.asarray(t) for t in tables)


def kernel(x, p, mix_pre_norm, w_in, conv_dw_w, conv_dw_b, conv_ln_w, conv_ln_b, w_out,
           mix_post_norm, ffn_pre_norm, w_ffn_gate, w_ffn_up, w_ffn_down, ffn_post_norm,
           w_ple_gate, w_ple_proj, ple_post_norm):
    batch, seq, d_model = x.shape
    depth = p.shape[0]
    assert depth == 1 and d_model == D_MODEL and seq % SEQ_TILE == 0 and SEQ_TILE % CHUNK == 0
    ts = SEQ_TILE
    tiles_per_row = seq // ts
    n_tiles = batch * tiles_per_row
    cosf, sinf, dmask, qd, kd, cd = _retention_tables(seq)

    row = lambda v: v.reshape(1, -1)
    hbm_spec = pl.BlockSpec(memory_space=pl.ANY)
    padded = lambda cols: cols + (LANES if (cols // LANES) % SUBLANES == 0 else 0)

    mixer_tile = lambda n: jnp.minimum(n, n_tiles - 1)
    channel_tile = lambda n: jnp.maximum(n - 1, 0)

    def const_spec(shape):
        zeros = (0,) * len(shape)
        return pl.BlockSpec(shape, lambda n: zeros, pipeline_mode=pl.Buffered(1))

    rope_spec = pl.BlockSpec((ts, HEAD_DIM), lambda n: (mixer_tile(n) % tiles_per_row, 0))
    x_flat = x.reshape(batch * seq, D_MODEL)
    operands = [
        (cd, pl.BlockSpec(memory_space=pltpu.SMEM)),
        (x_flat, pl.BlockSpec((ts, D_MODEL), lambda n: (mixer_tile(n), 0))),
        (x_flat, pl.BlockSpec((ts, D_MODEL), lambda n: (channel_tile(n), 0))),
        (p.reshape(batch * seq, D_PLE), pl.BlockSpec((ts, D_PLE), lambda n: (channel_tile(n), 0))),
        (cosf, rope_spec),
        (sinf, rope_spec),
        (dmask, const_spec((N_HEADS, CHUNK, CHUNK))),
        (qd, const_spec((CHUNK, D_RET))),
        (kd, const_spec((CHUNK, D_RET))),
        (row(mix_pre_norm[0]), const_spec((1, D_MODEL))),
        (w_in[0], hbm_spec),
        (jnp.broadcast_to(conv_dw_w[0][:, None, :], (CONV_WIDTH, SUBLANES, D_CONV)),
         const_spec((CONV_WIDTH, SUBLANES, D_CONV))),
        (jnp.broadcast_to(conv_dw_b[0][None, :], (SUBLANES, D_CONV)), const_spec((SUBLANES, D_CONV))),
        (row(conv_ln_w[0]), const_spec((1, D_CONV))),
        (row(conv_ln_b[0]), const_spec((1, D_CONV))),
        (w_out[0], hbm_spec),
        (row(mix_post_norm[0]), const_spec((1, D_MODEL))),
        (row(ffn_pre_norm[0]), const_spec((1, D_MODEL))),
        (w_ffn_gate[0], hbm_spec),
        (w_ffn_up[0], hbm_spec),
        (w_ffn_down[0], hbm_spec),
        (row(ffn_post_norm[0]), const_spec((1, D_MODEL))),
        (w_ple_gate[0], hbm_spec),
        (w_ple_proj[0], hbm_spec),
        (row(ple_post_norm[0]), const_spec((1, D_MODEL))),
    ]
    args = [a for a, _ in operands]
    in_specs = [s for _, s in operands]

    out = pl.pallas_call(
        functools.partial(_block_kernel, tiles_per_row),
        grid=(n_tiles + 1,),
        in_specs=in_specs,
        out_specs=pl.BlockSpec((ts, D_MODEL), lambda n: (channel_tile(n), 0)),
        out_shape=jax.ShapeDtypeStruct((batch * seq, D_MODEL), x.dtype),
        scratch_shapes=[
            pltpu.VMEM((ts, D_IN), F32),
            pltpu.VMEM((SUBLANES, HALO + ts, D_CONV), F32),
            pltpu.VMEM((ts, D_MODEL), BF16),
            pltpu.VMEM((ts, D_FF), BF16),
            pltpu.VMEM((N_HEADS, HEAD_DIM, HEAD_DIM), F32),
            pltpu.VMEM((D_MODEL, padded(D_IN)), BF16),
            pltpu.VMEM((D_MODEL, padded(D_MODEL)), BF16),
            pltpu.VMEM((D_MODEL, D_FF), BF16),
            pltpu.VMEM((D_MODEL, D_FF), BF16),
            pltpu.VMEM((D_FF, padded(D_MODEL)), BF16),
            pltpu.VMEM((D_MODEL, padded(D_MODEL)), BF16),
            pltpu.VMEM((D_PLE, padded(D_MODEL)), BF16),
            pltpu.VMEM((2, WEIGHT_ROWS, D_IN), F32),
            pltpu.SemaphoreType.DMA((2,)),
        ],
        compiler_params=pltpu.CompilerParams(
            dimension_semantics=("arbitrary",),
            vmem_limit_bytes=VMEM_LIMIT_BYTES,
        ),
        name="hymba_block",
    )(*args)
    return out.reshape(batch, seq, D_MODEL)
```

```python
import functools

import jax
import jax.numpy as jnp
import numpy as np
from jax import lax
from jax.experimental import pallas as pl
from jax.experimental.pallas import tpu as pltpu

D_MODEL = 1024
D_CONV = 512
D_RET = 512
N_HEADS = 4
HEAD_DIM = 128
CONV_WIDTH = 31
CHUNK = 128
D_FF = 2816
D_PLE = 256
D_IN = 2 * D_CONV + 4 * D_RET
ROPE_BASE = 10000.0
EPS = 1e-6

SUBLANES = 8
LANES = 128
SEQ_TILE = 256
HALO = 32
CONV_ROWS = 64
FFN_COLS = 256
DOWN_COLS = 256
WEIGHT_ROWS = 256
VMEM_LIMIT_BYTES = 56 * 1024 * 1024

SEGMENT_ORDER = (
    "CMMC "
    "CMCMCMCMCMCMCCCCC "
    "CCCC "
    "MC "
    "CMMMC"
)

Q0 = 2 * D_CONV
K0 = Q0 + D_RET
V0 = K0 + D_RET
G0 = V0 + D_RET

F32 = jnp.float32
BF16 = jnp.bfloat16
_EXHAUSTED = object()


def _rmsnorm(x, w):
    ms = jnp.mean(x * x, axis=-1, keepdims=True)
    return (x * lax.rsqrt(ms + EPS)) * w


def _silu(x):
    return x * jax.nn.sigmoid(x)


def _dot(a, b):
    return jnp.dot(a, b, preferred_element_type=F32)


def _load_weights_bf16(pairs, stage_ref, stage_sem):
    slabs = [(w_hbm, w_ref, r0) for w_hbm, w_ref in pairs
             for r0 in range(0, w_hbm.shape[0], WEIGHT_ROWS)]

    def slab_copy(k):
        w_hbm, _, r0 = slabs[k]
        cols = w_hbm.shape[1]
        return pltpu.make_async_copy(w_hbm.at[r0:r0 + WEIGHT_ROWS, :],
                                     stage_ref.at[k % 2, :, 0:cols], stage_sem.at[k % 2])

    slab_copy(0).start()
    for k, (w_hbm, w_ref, r0) in enumerate(slabs):
        if k + 1 < len(slabs):
            slab_copy(k + 1).start()
        slab_copy(k).wait()
        cols = w_hbm.shape[1]
        w_ref[r0:r0 + WEIGHT_ROWS, 0:cols] = stage_ref[k % 2, :, 0:cols].astype(BF16)


def _mixer_stage(cd_ref, x_ref, cos_ref, sin_ref, dmask_ref, qd_ref, kd_ref,
                 mix_pre_ref, w_in_ref, dw_w_ref, dw_b_ref, ln_w_ref, ln_b_ref,
                 u_ref, glu_ref, state_ref, mix_ref):
    ts = x_ref.shape[0]
    h = _rmsnorm(x_ref[...], mix_pre_ref[...]).astype(BF16)
    yield
    u_ref[:, 0:Q0] = _dot(h, w_in_ref[:, 0:Q0])
    yield
    glu_ref[0, HALO:HALO + ts, :] = u_ref[:, 0:D_CONV] * jax.nn.sigmoid(u_ref[:, D_CONV:2 * D_CONV])
    yield
    for b in range(1, SUBLANES):
        glu_ref[b, SUBLANES:HALO + ts, :] = glu_ref[0, SUBLANES - b:HALO + ts - b, :]
    yield
    groups = CONV_ROWS // SUBLANES
    for r0 in range(0, ts, CONV_ROWS):
        accs = [dw_b_ref[...]] * groups
        for b in range(SUBLANES):
            n_a = (CONV_WIDTH - 1 - b) // SUBLANES + 1
            base = HALO + r0 - SUBLANES * (n_a - 1)
            w_taps = [dw_w_ref[CONV_WIDTH - 1 - (SUBLANES * a + b)] for a in range(n_a)]
            for j in range(groups + n_a - 1):
                row = glu_ref[b, base + SUBLANES * j:base + SUBLANES * (j + 1), :]
                for a in range(n_a):
                    i = j - (n_a - 1) + a
                    if 0 <= i < groups:
                        accs[i] = accs[i] + row * w_taps[a]
        acc = jnp.concatenate(accs, axis=0)
        mu = jnp.mean(acc, axis=-1, keepdims=True)
        cen = acc - mu
        var = jnp.mean(cen * cen, axis=-1, keepdims=True)
        yn = (cen * lax.rsqrt(var + EPS)) * ln_w_ref[...] + ln_b_ref[...]
        mix_ref[r0:r0 + CONV_ROWS, 0:D_CONV] = _silu(yn).astype(BF16)
        if r0 + CONV_ROWS == ts:
            glu_ref[0, 0:HALO, :] = glu_ref[0, ts:ts + HALO, :]
        yield

    u_ref[:, Q0:D_IN] = _dot(h, w_in_ref[:, Q0:D_IN])
    yield
    scale = HEAD_DIM ** -0.5
    units = [(c0, hd) for c0 in range(0, ts, CHUNK) for hd in range(N_HEADS)]
    qbs, vs, scores, carried = {}, {}, {}, {}
    for c0, hd in units:
        lo = hd * HEAD_DIM
        cosf = cos_ref[c0:c0 + CHUNK, :]
        sinf = sin_ref[c0:c0 + CHUNK, :]
        q = u_ref[c0:c0 + CHUNK, Q0 + lo:Q0 + lo + HEAD_DIM]
        k = u_ref[c0:c0 + CHUNK, K0 + lo:K0 + lo + HEAD_DIM]
        v = u_ref[c0:c0 + CHUNK, V0 + lo:V0 + lo + HEAD_DIM].astype(BF16)
        q = q * cosf + pltpu.roll(q, HEAD_DIM // 2, 1) * sinf
        k = (k * cosf + pltpu.roll(k, HEAD_DIM // 2, 1) * sinf) * scale
        qb = q.astype(BF16)
        scores[c0, hd] = lax.dot_general(qb, k.astype(BF16), (((1,), (1,)), ((), ())),
                                         preferred_element_type=F32)
        kdec = (k * kd_ref[:, lo:lo + HEAD_DIM]).astype(BF16)
        kv = lax.dot_general(kdec, v, (((0,), (0,)), ((), ())), preferred_element_type=F32)
        state = state_ref[hd]
        carried[c0, hd] = state.astype(BF16)
        state_ref[hd] = state * cd_ref[hd] + kv
        qbs[c0, hd], vs[c0, hd] = qb, v
    yield
    outs = {}
    for c0, hd in units:
        lo = hd * HEAD_DIM
        sc = (scores[c0, hd] * dmask_ref[hd]).astype(BF16)
        inner = _dot(sc, vs[c0, hd])
        cross = _dot(qbs[c0, hd], carried[c0, hd]) * qd_ref[:, lo:lo + HEAD_DIM]
        outs[c0, hd] = inner + cross
    yield
    for c0, hd in units:
        lo = hd * HEAD_DIM
        o = outs[c0, hd]
        o = o * lax.rsqrt(jnp.mean(o * o, axis=-1, keepdims=True) + EPS)
        g = u_ref[c0:c0 + CHUNK, G0 + lo:G0 + lo + HEAD_DIM]
        mix_ref[c0:c0 + CHUNK, D_CONV + lo:D_CONV + lo + HEAD_DIM] = (_silu(g) * o).astype(BF16)
    yield


def _channel_stage(mix_ref, xc_ref, p_ref, w_out_ref, mix_post_ref, ffn_pre_ref, w_gate_ref,
                   w_up_ref, w_down_ref, ffn_post_ref, w_pg_ref, w_pp_ref, ple_post_ref,
                   o_ref, act_ref):
    y = _dot(mix_ref[...], w_out_ref[:, 0:D_MODEL])
    yield
    x = xc_ref[...] + _rmsnorm(y, mix_post_ref[...])
    h = _rmsnorm(x, ffn_pre_ref[...]).astype(BF16)
    yield
    for c0 in range(0, D_FF, FFN_COLS):
        gate = _dot(h, w_gate_ref[:, c0:c0 + FFN_COLS])
        up = _dot(h, w_up_ref[:, c0:c0 + FFN_COLS])
        act_ref[:, c0:c0 + FFN_COLS] = (_silu(gate) * up).astype(BF16)
        yield
    parts = []
    for c0 in range(0, D_MODEL, DOWN_COLS):
        parts.append(_dot(act_ref[...], w_down_ref[:, c0:c0 + DOWN_COLS]))
        yield
    x = x + _rmsnorm(jnp.concatenate(parts, axis=1), ffn_post_ref[...])
    yield
    e_gate = _dot(x.astype(BF16), w_pg_ref[:, 0:D_MODEL])
    e_proj = _dot(p_ref[...].astype(BF16), w_pp_ref[:, 0:D_MODEL])
    yield
    e = jax.nn.sigmoid(e_gate) * e_proj
    o_ref[...] = x + _rmsnorm(e, ple_post_ref[...])
    yield


def _block_kernel(tiles_per_row,
                  cd_ref, x_ref, xc_ref, p_ref, cos_ref, sin_ref, dmask_ref, qd_ref, kd_ref,
                  mix_pre_ref, w_in_hbm, dw_w_ref, dw_b_ref, ln_w_ref, ln_b_ref,
                  w_out_hbm, mix_post_ref, ffn_pre_ref, w_gate_hbm, w_up_hbm,
                  w_down_hbm, ffn_post_ref, w_pg_hbm, w_pp_hbm, ple_post_ref,
                  o_ref,
                  u_ref, glu_ref, mix_ref, act_ref, state_ref,
                  w_in_ref, w_out_ref, w_gate_ref, w_up_ref, w_down_ref, w_pg_ref, w_pp_ref,
                  stage_ref, stage_sem):
    n = pl.program_id(0)

    @pl.when(n == 0)
    def _():
        mix_ref[...] = jnp.zeros_like(mix_ref)
        _load_weights_bf16(((w_in_hbm, w_in_ref), (w_out_hbm, w_out_ref), (w_gate_hbm, w_gate_ref),
                            (w_up_hbm, w_up_ref), (w_down_hbm, w_down_ref), (w_pg_hbm, w_pg_ref),
                            (w_pp_hbm, w_pp_ref)), stage_ref, stage_sem)

    @pl.when(n % tiles_per_row == 0)
    def _():
        state_ref[...] = jnp.zeros_like(state_ref)
        glu_ref[0, 0:HALO, :] = jnp.zeros((HALO, D_CONV), F32)

    stages = {
        "C": _channel_stage(mix_ref, xc_ref, p_ref, w_out_ref, mix_post_ref, ffn_pre_ref,
                            w_gate_ref, w_up_ref, w_down_ref, ffn_post_ref, w_pg_ref, w_pp_ref,
                            ple_post_ref, o_ref, act_ref),
        "M": _mixer_stage(cd_ref, x_ref, cos_ref, sin_ref, dmask_ref, qd_ref, kd_ref,
                          mix_pre_ref, w_in_ref, dw_w_ref, dw_b_ref, ln_w_ref, ln_b_ref,
                          u_ref, glu_ref, state_ref, mix_ref),
    }
    for key in SEGMENT_ORDER.replace(" ", ""):
        next(stages[key])
    for stage in stages.values():
        assert next(stage, _EXHAUSTED) is _EXHAUSTED, "SEGMENT_ORDER leaves a stage unfinished"


def _retention_tables(seq):
    f32 = np.float32
    half = HEAD_DIM // 2
    pos = np.arange(seq, dtype=f32)
    freqs = (f32(ROPE_BASE) ** (-np.arange(half, dtype=f32) / f32(half))).astype(f32)
    ang = (pos[:, None] * freqs[None, :]).astype(f32)
    cos, sin = np.cos(ang).astype(f32), np.sin(ang).astype(f32)
    cosf = np.concatenate([cos, cos], axis=-1)
    sinf = np.concatenate([-sin, sin], axis=-1)
    log_g = np.log(f32(1.0) - f32(2.0) ** (f32(-5.0) - np.arange(N_HEADS, dtype=f32))).astype(f32)
    idx = np.arange(CHUNK, dtype=f32)
    rel = idx[:, None] - idx[None, :]
    dmask = np.where(rel >= 0, np.exp(log_g[:, None, None] * np.maximum(rel, f32(0.0))), f32(0.0)).astype(f32)
    q_decay = np.exp(log_g[:, None] * (idx + f32(1.0))).astype(f32)
    k_decay = np.exp(log_g[:, None] * (f32(CHUNK - 1.0) - idx)).astype(f32)
    chunk_decay = np.exp(log_g * f32(CHUNK)).astype(f32)
    widen = lambda t: np.repeat(t.T, HEAD_DIM, axis=1)
    tables = (cosf, sinf, dmask, widen(q_decay), widen(k_decay), chunk_decay)
    return tuple(jnp.asarray(t) for t in tables)


def kernel(x, p, mix_pre_norm, w_in, conv_dw_w, conv_dw_b, conv_ln_w, conv_ln_b, w_out,
           mix_post_norm, ffn_pre_norm, w_ffn_gate, w_ffn_up, w_ffn_down, ffn_post_norm,
           w_ple_gate, w_ple_proj, ple_post_norm):
    batch, seq, d_model = x.shape
    depth = p.shape[0]
    assert depth == 1 and d_model == D_MODEL and seq % SEQ_TILE == 0 and SEQ_TILE % CHUNK == 0
    ts = SEQ_TILE
    tiles_per_row = seq // ts
    n_tiles = batch * tiles_per_row
    cosf, sinf, dmask, qd, kd, cd = _retention_tables(seq)

    row = lambda v: v.reshape(1, -1)
    hbm_spec = pl.BlockSpec(memory_space=pl.ANY)
    padded = lambda cols: cols + (LANES if (cols // LANES) % SUBLANES == 0 else 0)

    mixer_tile = lambda n: jnp.minimum(n, n_tiles - 1)
    channel_tile = lambda n: jnp.maximum(n - 1, 0)

    def const_spec(shape):
        zeros = (0,) * len(shape)
        return pl.BlockSpec(shape, lambda n: zeros, pipeline_mode=pl.Buffered(1))

    rope_spec = pl.BlockSpec((ts, HEAD_DIM), lambda n: (mixer_tile(n) % tiles_per_row, 0))
    x_flat = x.reshape(batch * seq, D_MODEL)
    operands = [
        (cd, pl.BlockSpec(memory_space=pltpu.SMEM)),
        (x_flat, pl.BlockSpec((ts, D_MODEL), lambda n: (mixer_tile(n), 0))),
        (x_flat, pl.BlockSpec((ts, D_MODEL), lambda n: (channel_tile(n), 0))),
        (p.reshape(batch * seq, D_PLE), pl.BlockSpec((ts, D_PLE), lambda n: (channel_tile(n), 0))),
        (cosf, rope_spec),
        (sinf, rope_spec),
        (dmask, const_spec((N_HEADS, CHUNK, CHUNK))),
        (qd, const_spec((CHUNK, D_RET))),
        (kd, const_spec((CHUNK, D_RET))),
        (row(mix_pre_norm[0]), const_spec((1, D_MODEL))),
        (w_in[0], hbm_spec),
        (jnp.broadcast_to(conv_dw_w[0][:, None, :], (CONV_WIDTH, SUBLANES, D_CONV)),
         const_spec((CONV_WIDTH, SUBLANES, D_CONV))),
        (jnp.broadcast_to(conv_dw_b[0][None, :], (SUBLANES, D_CONV)), const_spec((SUBLANES, D_CONV))),
        (row(conv_ln_w[0]), const_spec((1, D_CONV))),
        (row(conv_ln_b[0]), const_spec((1, D_CONV))),
        (w_out[0], hbm_spec),
        (row(mix_post_norm[0]), const_spec((1, D_MODEL))),
        (row(ffn_pre_norm[0]), const_spec((1, D_MODEL))),
        (w_ffn_gate[0], hbm_spec),
        (w_ffn_up[0], hbm_spec),
        (w_ffn_down[0], hbm_spec),
        (row(ffn_post_norm[0]), const_spec((1, D_MODEL))),
        (w_ple_gate[0], hbm_spec),
        (w_ple_proj[0], hbm_spec),
        (row(ple_post_norm[0]), const_spec((1, D_MODEL))),
    ]
    args = [a for a, _ in operands]
    in_specs = [s for _, s in operands]

    out = pl.pallas_call(
        functools.partial(_block_kernel, tiles_per_row),
        grid=(n_tiles + 1,),
        in_specs=in_specs,
        out_specs=pl.BlockSpec((ts, D_MODEL), lambda n: (channel_tile(n), 0)),
        out_shape=jax.ShapeDtypeStruct((batch * seq, D_MODEL), x.dtype),
        scratch_shapes=[
            pltpu.VMEM((ts, D_IN), F32),
            pltpu.VMEM((SUBLANES, HALO + ts, D_CONV), F32),
            pltpu.VMEM((ts, D_MODEL), BF16),
            pltpu.VMEM((ts, D_FF), BF16),
            pltpu.VMEM((N_HEADS, HEAD_DIM, HEAD_DIM), F32),
            pltpu.VMEM((D_MODEL, padded(D_IN)), BF16),
            pltpu.VMEM((D_MODEL, padded(D_MODEL)), BF16),
            pltpu.VMEM((D_MODEL, D_FF), BF16),
            pltpu.VMEM((D_MODEL, D_FF), BF16),
            pltpu.VMEM((D_FF, padded(D_MODEL)), BF16),
            pltpu.VMEM((D_MODEL, padded(D_MODEL)), BF16),
            pltpu.VMEM((D_PLE, padded(D_MODEL)), BF16),
            pltpu.VMEM((2, WEIGHT_ROWS, D_IN), F32),
            pltpu.SemaphoreType.DMA((2,)),
        ],
        compiler_params=pltpu.CompilerParams(
            dimension_semantics=("arbitrary",),
            vmem_limit_bytes=VMEM_LIMIT_BYTES,
        ),
        name="hymba_block",
    )(*args)
    return out.reshape(batch, seq, D_MODEL)
```
